```python
import math
import jax, jax.numpy as jnp
from jax import lax
import numpy as np

D_MODEL = 1024
BATCH = 8
SEQ = 2048
DEPTH = 1
DEC_BATCH = 32
DEC_SEQ = 16
PAST_LEN = 2048

CHUNK = 64
S5_GROUP = 16
S5_GROUPS = 32
S5_WIDTH = S5_GROUP * S5_GROUPS
S5_STATE = 64
LRU_WIDTH = D_MODEL
LRU_HEADS = 16
LRU_HEAD_DIM = LRU_WIDTH // LRU_HEADS
LRU_C = 8.0
CONV_W = 4
IN_COLS = 2 * S5_WIDTH + 2 * LRU_WIDTH + 2 * D_MODEL
EPS = 1e-6
DT_MIN = 1e-3
DT_MAX = 1e-1

kernel_name = 'hybrid_s5_rglru_stream_step'


def _f32(a):
    return a.astype(jnp.float32)


def rmsnorm(x, g):
    xf = _f32(x)
    return xf * lax.rsqrt(jnp.mean(xf * xf, axis=-1, keepdims=True) + EPS) * _f32(g)


def _complex_affine_combine(e1, e2):
    ar1, ai1, br1, bi1 = e1
    ar2, ai2, br2, bi2 = e2
    return (ar2 * ar1 - ai2 * ai1,
            ar2 * ai1 + ai2 * ar1,
            ar2 * br1 - ai2 * bi1 + br2,
            ar2 * bi1 + ai2 * br1 + bi2)


def s5_mixer(u, h0, lam_re, lam_im, log_dt, b_re, b_im, c_re, c_im, d):
    bt, L, _ = u.shape
    ug = u.reshape(bt, L, S5_GROUPS, S5_GROUP)
    lam_re = _f32(lam_re)
    lam_im = _f32(lam_im)
    dt = jnp.exp(_f32(log_dt))[:, None]
    mag = jnp.exp(lam_re * dt)
    ang = lam_im * dt
    lb_re = mag * jnp.cos(ang)
    lb_im = mag * jnp.sin(ang)
    den = lam_re * lam_re + lam_im * lam_im
    nr = lb_re - 1.0
    coef_re = (nr * lam_re + lb_im * lam_im) / den
    coef_im = (lb_im * lam_re - nr * lam_im) / den
    bu_re = jnp.einsum('blgh,gph->blgp', ug, _f32(b_re))
    bu_im = jnp.einsum('blgh,gph->blgp', ug, _f32(b_im))
    x_re = coef_re * bu_re - coef_im * bu_im
    x_im = coef_re * bu_im + coef_im * bu_re
    a_re = jnp.broadcast_to(lb_re, x_re.shape)
    a_im = jnp.broadcast_to(lb_im, x_im.shape)
    A_re, A_im, H_re, H_im = lax.associative_scan(
        _complex_affine_combine, (a_re, a_im, x_re, x_im), axis=1)
    h0f = _f32(h0)
    h0_re = h0f[:, None, :, :, 0]
    h0_im = h0f[:, None, :, :, 1]
    s_re = H_re + A_re * h0_re - A_im * h0_im
    s_im = H_im + A_re * h0_im + A_im * h0_re
    y = (jnp.einsum('blgp,ghp->blgh', s_re, _f32(c_re))
         - jnp.einsum('blgp,ghp->blgh', s_im, _f32(c_im))
         + ug * _f32(d))
    h_last = jnp.stack([s_re[:, -1], s_im[:, -1]], axis=-1)
    return y.reshape(bt, L, S5_WIDTH), h_last


def causal_conv(u, buf, w, b):
    L = u.shape[1]
    padded = jnp.concatenate([_f32(buf), u], axis=1)
    wf = _f32(w)
    y = _f32(b)
    for k in range(CONV_W):
        y = y + padded[:, k:k + L] * wf[k]
    return y, padded[:, -(CONV_W - 1):]


def rglru(xc, h0, wa, ba, wx, bx, lam):
    bt, L, _ = xc.shape
    xh = xc.reshape(bt, L, LRU_HEADS, LRU_HEAD_DIM)
    r = jax.nn.sigmoid(jnp.einsum('blhi,hij->blhj', xh, _f32(wa)) + _f32(ba)).reshape(bt, L, LRU_WIDTH)
    i = jax.nn.sigmoid(jnp.einsum('blhi,hij->blhj', xh, _f32(wx)) + _f32(bx)).reshape(bt, L, LRU_WIDTH)
    log_a = -LRU_C * r * jax.nn.softplus(-_f32(lam))
    a = jnp.exp(log_a)
    g = jnp.sqrt(jnp.maximum(-jnp.expm1(2.0 * log_a), 0.0)) * (i * xc)

    def step(h, inp):
        a_t, g_t = inp
        h = a_t * h + g_t
        return h, h

    h_last, hs = lax.scan(step, _f32(h0), (jnp.swapaxes(a, 0, 1), jnp.swapaxes(g, 0, 1)))
    return jnp.swapaxes(hs, 0, 1), h_last


def hybrid_layer(x, s5_h0, lru_h0, conv_buf, ln_g, w_in, lam_re, lam_im, log_dt, b_re, b_im,
                 c_re, c_im, d, w_glu, b_glu, conv_w, conv_b, wa, ba, wx, bx, lam,
                 w_pa, w_pb, w_out):
    h = rmsnorm(x, ln_g)
    proj = h @ _f32(w_in)
    cuts = [S5_WIDTH, 2 * S5_WIDTH, 2 * S5_WIDTH + LRU_WIDTH,
            2 * S5_WIDTH + 2 * LRU_WIDTH, 2 * S5_WIDTH + 2 * LRU_WIDTH + D_MODEL]
    u_a, z_a, u_b, z_b, g_a, g_b = jnp.split(proj, cuts, axis=-1)
    y_a, s5_new = s5_mixer(u_a, s5_h0, lam_re, lam_im, log_dt, b_re, b_im, c_re, c_im, d)
    y_a = jax.nn.gelu(y_a)
    y_a = y_a * jax.nn.sigmoid(y_a @ _f32(w_glu) + _f32(b_glu))
    p_a = (y_a * jax.nn.silu(z_a)) @ _f32(w_pa)
    xc, conv_new = causal_conv(u_b, conv_buf, conv_w, conv_b)
    y_b, lru_new = rglru(xc, lru_h0, wa, ba, wx, bx, lam)
    p_b = (y_b * jax.nn.silu(z_b)) @ _f32(w_pb)
    m = jax.nn.sigmoid(g_a) * p_a + jax.nn.sigmoid(g_b) * p_b
    out = _f32(x) + m @ _f32(w_out)
    return out, s5_new, lru_new, conv_new


def setup_inputs(seed: int = 0) -> dict:
    key = jax.random.key(seed)
    ks = jax.random.split(key, 32)
    f = jnp.float32
    nrm = lambda k, s, sc: sc * jax.random.normal(k, s, f)
    x_prompt = nrm(ks[0], (BATCH, SEQ, D_MODEL), 1.0)
    x_sample = nrm(ks[1], (DEC_BATCH, DEC_SEQ, D_MODEL), 1.0)
    state_s5 = nrm(ks[2], (DEPTH, DEC_BATCH, S5_GROUPS, S5_STATE, 2), 0.3)
    state_lru = nrm(ks[3], (DEPTH, DEC_BATCH, LRU_WIDTH), 0.5)
    state_conv = nrm(ks[4], (DEPTH, DEC_BATCH, CONV_W - 1, LRU_WIDTH), 1.0)
    ln_gain = 1.0 + nrm(ks[5], (DEPTH, D_MODEL), 0.05)
    w_in = nrm(ks[6], (DEPTH, D_MODEL, IN_COLS), D_MODEL ** -0.5)
    s5_lambda_re = -0.5 * jnp.exp(nrm(ks[7], (DEPTH, S5_GROUPS, S5_STATE), 0.05))
    s5_lambda_im = (math.pi * jnp.arange(S5_STATE, dtype=f)[None, None, :]
                    + nrm(ks[8], (DEPTH, S5_GROUPS, S5_STATE), 0.02))
    s5_log_dt = jax.random.uniform(ks[9], (DEPTH, S5_GROUPS), f,
                                   math.log(DT_MIN), math.log(DT_MAX))
    s5_b_re = nrm(ks[10], (DEPTH, S5_GROUPS, S5_STATE, S5_GROUP), (2.0 * S5_GROUP) ** -0.5)
    s5_b_im = nrm(ks[11], (DEPTH, S5_GROUPS, S5_STATE, S5_GROUP), (2.0 * S5_GROUP) ** -0.5)
    s5_c_re = nrm(ks[12], (DEPTH, S5_GROUPS, S5_GROUP, S5_STATE), S5_STATE ** -0.5)
    s5_c_im = nrm(ks[13], (DEPTH, S5_GROUPS, S5_GROUP, S5_STATE), S5_STATE ** -0.5)
    s5_d = nrm(ks[14], (DEPTH, S5_GROUPS, S5_GROUP), 1.0)
    w_glu = nrm(ks[15], (DEPTH, S5_WIDTH, S5_WIDTH), S5_WIDTH ** -0.5)
    b_glu = nrm(ks[16], (DEPTH, S5_WIDTH), 0.01)
    conv_w = nrm(ks[17], (DEPTH, CONV_W, LRU_WIDTH), CONV_W ** -0.5)
    conv_b = nrm(ks[18], (DEPTH, LRU_WIDTH), 0.01)
    lru_wa = nrm(ks[19], (DEPTH, LRU_HEADS, LRU_HEAD_DIM, LRU_HEAD_DIM), LRU_HEAD_DIM ** -0.5)
    lru_ba = nrm(ks[20], (DEPTH, LRU_HEADS, LRU_HEAD_DIM), 0.01)
    lru_wx = nrm(ks[21], (DEPTH, LRU_HEADS, LRU_HEAD_DIM, LRU_HEAD_DIM), LRU_HEAD_DIM ** -0.5)
    lru_bx = nrm(ks[22], (DEPTH, LRU_HEADS, LRU_HEAD_DIM), 0.01)
    a_c = jax.random.uniform(ks[23], (DEPTH, LRU_WIDTH), f, 0.9, 0.999)
    a_base = a_c ** (1.0 / LRU_C)
    lru_lambda = jnp.log(a_base) - jnp.log1p(-a_base)
    w_pa = nrm(ks[24], (DEPTH, S5_WIDTH, D_MODEL), S5_WIDTH ** -0.5)
    w_pb = nrm(ks[25], (DEPTH, LRU_WIDTH, D_MODEL), LRU_WIDTH ** -0.5)
    w_out = nrm(ks[26], (DEPTH, D_MODEL, D_MODEL), D_MODEL ** -0.5)
    final_gain = 1.0 + nrm(ks[27], (D_MODEL,), 0.05)
    return {'x_prompt': x_prompt, 'x_sample': x_sample, 'state_s5': state_s5,
            'state_lru': state_lru, 'state_conv': state_conv, 'ln_gain': ln_gain,
            'w_in': w_in, 's5_lambda_re': s5_lambda_re, 's5_lambda_im': s5_lambda_im,
            's5_log_dt': s5_log_dt, 's5_b_re': s5_b_re, 's5_b_im': s5_b_im,
            's5_c_re': s5_c_re, 's5_c_im': s5_c_im, 's5_d': s5_d, 'w_glu': w_glu,
            'b_glu': b_glu, 'conv_w': conv_w, 'conv_b': conv_b, 'lru_wa': lru_wa,
            'lru_ba': lru_ba, 'lru_wx': lru_wx, 'lru_bx': lru_bx, 'lru_lambda': lru_lambda,
            'w_pa': w_pa, 'w_pb': w_pb, 'w_out': w_out, 'final_gain': final_gain}


def reference(x_prompt, x_sample, state_s5, state_lru, state_conv, ln_gain, w_in,
              s5_lambda_re, s5_lambda_im, s5_log_dt, s5_b_re, s5_b_im, s5_c_re, s5_c_im,
              s5_d, w_glu, b_glu, conv_w, conv_b, lru_wa, lru_ba, lru_wx, lru_bx,
              lru_lambda, w_pa, w_pb, w_out, final_gain):
    def run_group(x, s5_init, lru_init, conv_init):
        h = _f32(x)
        s5_out, lru_out, conv_out = [], [], []
        for l in range(DEPTH):
            h, s5_n, lru_n, conv_n = hybrid_layer(
                h, s5_init[l], lru_init[l], conv_init[l], ln_gain[l], w_in[l],
                s5_lambda_re[l], s5_lambda_im[l], s5_log_dt[l], s5_b_re[l], s5_b_im[l],
                s5_c_re[l], s5_c_im[l], s5_d[l], w_glu[l], b_glu[l], conv_w[l], conv_b[l],
                lru_wa[l], lru_ba[l], lru_wx[l], lru_bx[l], lru_lambda[l],
                w_pa[l], w_pb[l], w_out[l])
            s5_out.append(s5_n)
            lru_out.append(lru_n)
            conv_out.append(conv_n)
        y = rmsnorm(h, final_gain).astype(x.dtype)
        return y, jnp.stack(s5_out), jnp.stack(lru_out), jnp.stack(conv_out)

    sdt = state_s5.dtype
    zs5 = [jnp.zeros((BATCH, S5_GROUPS, S5_STATE, 2), jnp.float32)] * DEPTH
    zlru = [jnp.zeros((BATCH, LRU_WIDTH), jnp.float32)] * DEPTH
    zconv = [jnp.zeros((BATCH, CONV_W - 1, LRU_WIDTH), jnp.float32)] * DEPTH
    y_prompt, s5_p, lru_p, conv_p = run_group(x_prompt, zs5, zlru, zconv)
    y_sample, s5_s, lru_s, conv_s = run_group(
        x_sample, [state_s5[l] for l in range(DEPTH)], [state_lru[l] for l in range(DEPTH)],
        [state_conv[l] for l in range(DEPTH)])
    return (y_prompt, y_sample,
            s5_p.astype(sdt), lru_p.astype(state_lru.dtype), conv_p.astype(state_conv.dtype),
            s5_s.astype(sdt), lru_s.astype(state_lru.dtype), conv_s.astype(state_conv.dtype))
```

```python
import functools
import math

import jax
import jax.numpy as jnp
from jax import lax
from jax.experimental import pallas as pl
from jax.experimental.pallas import tpu as pltpu

D_MODEL = 1024
S5_GROUP = 16
S5_GROUPS = 32
S5_WIDTH = S5_GROUP * S5_GROUPS
S5_STATE = 64
LRU_WIDTH = D_MODEL
LRU_HEADS = 16
LRU_HEAD_DIM = LRU_WIDTH // LRU_HEADS
LRU_C = 8.0
CONV_W = 4
EPS = 1e-6

SUBLANES = 8
LANES = 128
MXU_DIM = 256
N_QUADS = S5_WIDTH // LANES
QUAD_GROUPS = S5_GROUPS // N_QUADS
QUAD_STATE = QUAD_GROUPS * S5_STATE
S5_COLS = 2 * S5_GROUPS * S5_STATE
GATE_TILES = LRU_WIDTH // MXU_DIM
HEADS_PER_TILE = MXU_DIM // LRU_HEAD_DIM
N_SLABS = D_MODEL // LANES
VMEM_LIMIT_BYTES = 56 * 1024 * 1024

C_UA, C_ZA, C_UB, C_ZB, C_GA, C_GB = 0, 512, 1024, 2048, 3072, 4096


def _bf16(a):
    return a.astype(jnp.bfloat16)


def _dot(a, b):
    return jnp.dot(a, b, preferred_element_type=jnp.float32)


def _prep_kernel(lam_re_ref, lam_im_ref, log_dt_ref, b_re_ref, b_im_ref, c_im_ref,
                 lru_lam_ref, lb_re_ref, lb_im_ref, bb_re_ref, bb_im_ref,
                 c_imn_ref, c_lru_ref):
    lam_re = lam_re_ref[...]
    lam_im = lam_im_ref[...]
    dt = jnp.exp(log_dt_ref[...])
    mag = jnp.exp(lam_re * dt)
    ang = lam_im * dt
    lb_re = mag * jnp.cos(ang)
    lb_im = mag * jnp.sin(ang)
    den = lam_re * lam_re + lam_im * lam_im
    nr = lb_re - 1.0
    coef_re = (nr * lam_re + lb_im * lam_im) / den
    coef_im = (lb_im * lam_re - nr * lam_im) / den
    b_re = b_re_ref[...]
    b_im = b_im_ref[...]
    lb_re_ref[...] = lb_re
    lb_im_ref[...] = lb_im
    bb_re_ref[...] = coef_re * b_re - coef_im * b_im
    bb_im_ref[...] = coef_re * b_im + coef_im * b_re
    c_imn_ref[...] = -c_im_ref[...]
    nl = -lru_lam_ref[...]
    sp = jnp.maximum(nl, 0.0) + jnp.log1p(jnp.exp(-jnp.abs(nl)))
    c_lru_ref[...] = -LRU_C * sp


def _prep(lam_re, lam_im, log_dt, b_re, b_im, c_im, lru_lam):
    g, p, h = S5_GROUPS, S5_STATE, S5_GROUP
    f = jnp.float32
    outs = pl.pallas_call(
        _prep_kernel,
        out_shape=(jax.ShapeDtypeStruct((g, 1, p), f), jax.ShapeDtypeStruct((g, 1, p), f),
                   jax.ShapeDtypeStruct((g, h, p), f), jax.ShapeDtypeStruct((g, h, p), f),
                   jax.ShapeDtypeStruct((g, h, p), f), jax.ShapeDtypeStruct((1, LRU_WIDTH), f)),
        name="s5_lru_param_prep",
    )(lam_re.reshape(g, 1, p), lam_im.reshape(g, 1, p), log_dt.reshape(g, 1, 1),
      jnp.swapaxes(b_re, 1, 2), jnp.swapaxes(b_im, 1, 2), c_im, lru_lam.reshape(1, LRU_WIDTH))
    return outs


def _layer_kernel(x_ref, s5_0_ref, lru_0_ref, conv_0_ref, lng_ref, w_in_ref, lb_ref,
                  bc_ref, cc_ref, d_ref, w_glu_ref, b_glu_ref, conv_w_ref, conv_b_ref,
                  wa_ref, ba_ref, wx_ref, bx_ref, c_lru_ref, w_pa_ref, w_pb_ref,
                  w_out_ref, fg_ref,
                  y_ref, s5_n_ref, lru_n_ref, conv_n_ref,
                  tr_ref, hb_ref, ws_ref, cv_ref, hs5_ref, hl_ref, *, nb, t_len):
    tile_rows = t_len * SUBLANES
    tail_rows = (CONV_W - 1) * SUBLANES
    k = pl.program_id(1)

    @pl.when(k == 0)
    def _init_state():
        hs5_ref[...] = s5_0_ref[...]
        hl_ref[...] = lru_0_ref[...]
        for j in range(nb):
            cv_ref[j, 0:tail_rows, :] = conv_0_ref[:, j * SUBLANES:(j + 1) * SUBLANES, :].reshape(
                tail_rows, LRU_WIDTH)

    for jb in range(nb * SUBLANES):
        j, b = divmod(jb, SUBLANES)
        xb = x_ref[jb]
        inv = lax.rsqrt(jnp.mean(xb * xb, axis=-1, keepdims=True) + EPS)
        hn = xb * inv * lng_ref[...]
        for c in range(N_SLABS):
            tr_ref[c, pl.ds(j * tile_rows + b, t_len, stride=SUBLANES), :] = (
                hn[:, c * LANES:(c + 1) * LANES])
    for c in range(N_SLABS):
        hb_ref[:, c * LANES:(c + 1) * LANES] = _bf16(tr_ref[c])
    hb = hb_ref[...]

    def proj(col, width):
        return _dot(hb, w_in_ref[:, col:col + width])

    ua = proj(C_UA, S5_WIDTH)
    ua_b = _bf16(ua)
    for q in range(N_QUADS):
        ws_ref[:, q * 2 * QUAD_STATE:(q + 1) * 2 * QUAD_STATE] = _dot(
            ua_b[:, q * LANES:(q + 1) * LANES], bc_ref[q])

    for j in range(nb):
        for q in range(N_QUADS):
            re0 = q * 2 * QUAD_STATE
            im0 = re0 + QUAD_STATE
            lbr = jnp.broadcast_to(lb_ref[:, re0:re0 + QUAD_STATE], (SUBLANES, QUAD_STATE))
            lbi = jnp.broadcast_to(lb_ref[:, im0:im0 + QUAD_STATE], (SUBLANES, QUAD_STATE))
            hr0 = hs5_ref[j * SUBLANES:(j + 1) * SUBLANES, re0:re0 + QUAD_STATE]
            hi0 = hs5_ref[j * SUBLANES:(j + 1) * SUBLANES, im0:im0 + QUAD_STATE]

            def s5_step(t, carry, j=j, re0=re0, im0=im0, lbr=lbr, lbi=lbi):
                hr, hi = carry
                row = pl.multiple_of(j * tile_rows + t * SUBLANES, SUBLANES)
                xr = ws_ref[pl.ds(row, SUBLANES), re0:re0 + QUAD_STATE]
                xi = ws_ref[pl.ds(row, SUBLANES), im0:im0 + QUAD_STATE]
                nr = lbr * hr - lbi * hi + xr
                ni = lbr * hi + lbi * hr + xi
                ws_ref[pl.ds(row, SUBLANES), re0:re0 + QUAD_STATE] = nr
                ws_ref[pl.ds(row, SUBLANES), im0:im0 + QUAD_STATE] = ni
                return nr, ni

            hr, hi = lax.fori_loop(0, t_len, s5_step, (hr0, hi0), unroll=2)
            hs5_ref[j * SUBLANES:(j + 1) * SUBLANES, re0:re0 + QUAD_STATE] = hr
            hs5_ref[j * SUBLANES:(j + 1) * SUBLANES, im0:im0 + QUAD_STATE] = hi
    s5_n_ref[...] = hs5_ref[...]

    ya_parts = []
    for q in range(N_QUADS):
        sq = _bf16(ws_ref[:, q * 2 * QUAD_STATE:(q + 1) * 2 * QUAD_STATE])
        ya_parts.append(_dot(sq, cc_ref[q]))
    ya = jnp.concatenate(ya_parts, axis=-1) + ua * d_ref[...]
    ya = jax.nn.gelu(ya)
    ya = ya * jax.nn.sigmoid(_dot(_bf16(ya), w_glu_ref[...]) + b_glu_ref[...])
    za = proj(C_ZA, S5_WIDTH)
    pa = _dot(_bf16(ya * (za * jax.nn.sigmoid(za))), w_pa_ref[...])
    ga = proj(C_GA, D_MODEL)
    m_cols = slice(2 * LRU_WIDTH, 3 * LRU_WIDTH)
    ws_ref[:, m_cols] = jax.nn.sigmoid(ga) * pa

    ub = proj(C_UB, LRU_WIDTH)
    a_cols = slice(0, LRU_WIDTH)
    g_cols = slice(LRU_WIDTH, 2 * LRU_WIDTH)
    for j in range(nb):
        cv_ref[j, tail_rows:tail_rows + tile_rows, :] = ub[j * tile_rows:(j + 1) * tile_rows]
        xc = conv_b_ref[...] + conv_w_ref[0:1, :] * cv_ref[j, 0:tile_rows, :]
        for kk in range(1, CONV_W):
            xc = xc + conv_w_ref[kk:kk + 1, :] * cv_ref[j, kk * SUBLANES:kk * SUBLANES + tile_rows, :]
        new_tail = cv_ref[j, tile_rows:tile_rows + tail_rows, :]
        cv_ref[j, 0:tail_rows, :] = new_tail
        conv_n_ref[:, j * SUBLANES:(j + 1) * SUBLANES, :] = new_tail.reshape(
            CONV_W - 1, SUBLANES, LRU_WIDTH)
        xc_b = _bf16(xc)
        for gt in range(GATE_TILES):
            cs = slice(gt * MXU_DIM, (gt + 1) * MXU_DIM)
            r = jax.nn.sigmoid(_dot(xc_b[:, cs], wa_ref[gt]) + ba_ref[:, cs])
            i = jax.nn.sigmoid(_dot(xc_b[:, cs], wx_ref[gt]) + bx_ref[:, cs])
            log_a = c_lru_ref[:, cs] * r
            th = jnp.tanh(log_a)
            gain = jnp.sqrt(jnp.maximum(-2.0 * th / (1.0 - th), 0.0))
            rows = slice(j * tile_rows, (j + 1) * tile_rows)
            ws_ref[rows, gt * MXU_DIM:(gt + 1) * MXU_DIM] = jnp.exp(log_a)
            ws_ref[rows, LRU_WIDTH + gt * MXU_DIM:LRU_WIDTH + (gt + 1) * MXU_DIM] = (
                gain * (i * xc[:, cs]))

        def lru_step(t, h, j=j):
            row = pl.multiple_of(j * tile_rows + t * SUBLANES, SUBLANES)
            h = ws_ref[pl.ds(row, SUBLANES), a_cols] * h + ws_ref[pl.ds(row, SUBLANES), g_cols]
            ws_ref[pl.ds(row, SUBLANES), g_cols] = h
            return h

        hl = lax.fori_loop(0, t_len, lru_step, hl_ref[j * SUBLANES:(j + 1) * SUBLANES, :],
                           unroll=2)
        hl_ref[j * SUBLANES:(j + 1) * SUBLANES, :] = hl
    lru_n_ref[...] = hl_ref[...]

    zb = proj(C_ZB, LRU_WIDTH)
    pb = _dot(_bf16(ws_ref[:, g_cols] * (zb * jax.nn.sigmoid(zb))), w_pb_ref[...])
    gb = proj(C_GB, D_MODEL)
    m = ws_ref[:, m_cols] + jax.nn.sigmoid(gb) * pb
    delta = _dot(_bf16(m), w_out_ref[...])

    for c in range(N_SLABS):
        tr_ref[c] = delta[:, c * LANES:(c + 1) * LANES]
    for jb in range(nb * SUBLANES):
        j, b = divmod(jb, SUBLANES)
        parts = [tr_ref[c, pl.ds(j * tile_rows + b, t_len, stride=SUBLANES), :]
                 for c in range(N_SLABS)]
        o = x_ref[jb] + jnp.concatenate(parts, axis=-1)
        inv = lax.rsqrt(jnp.mean(o * o, axis=-1, keepdims=True) + EPS)
        y_ref[jb] = o * inv * fg_ref[...]


def _full_spec(arr):
    zeros = (0,) * arr.ndim
    return pl.BlockSpec(arr.shape, lambda i, k, zeros=zeros: zeros)


def _run_group(x, s5_0, lru_0, conv_0, weights, *, nb, t_len):
    bsz, seq, _ = x.shape
    bt = nb * SUBLANES
    assert bsz % bt == 0 and seq % t_len == 0 and t_len % SUBLANES == 0
    rows = bt * t_len
    grid = (bsz // bt, seq // t_len)
    f = jnp.float32
    in_specs = [
        pl.BlockSpec((bt, t_len, D_MODEL), lambda i, k: (i, k, 0)),
        pl.BlockSpec((bt, S5_COLS), lambda i, k: (i, 0)),
        pl.BlockSpec((bt, LRU_WIDTH), lambda i, k: (i, 0)),
        pl.BlockSpec((CONV_W - 1, bt, LRU_WIDTH), lambda i, k: (0, i, 0)),
    ] + [_full_spec(w) for w in weights]
    out_specs = (
        pl.BlockSpec((bt, t_len, D_MODEL), lambda i, k: (i, k, 0)),
        pl.BlockSpec((bt, S5_COLS), lambda i, k: (i, 0)),
        pl.BlockSpec((bt, LRU_WIDTH), lambda i, k: (i, 0)),
        pl.BlockSpec((CONV_W - 1, bt, LRU_WIDTH), lambda i, k: (0, i, 0)),
    )
    out_shape = (
        jax.ShapeDtypeStruct((bsz, seq, D_MODEL), x.dtype),
        jax.ShapeDtypeStruct((bsz, S5_COLS), f),
        jax.ShapeDtypeStruct((bsz, LRU_WIDTH), f),
        jax.ShapeDtypeStruct((CONV_W - 1, bsz, LRU_WIDTH), f),
    )
    scratch = [
        pltpu.VMEM((N_SLABS, rows, LANES), f),
        pltpu.VMEM((rows, D_MODEL), jnp.bfloat16),
        pltpu.VMEM((rows, S5_COLS), f),
        pltpu.VMEM((nb, (t_len + CONV_W - 1) * SUBLANES, LRU_WIDTH), f),
        pltpu.VMEM((bt, S5_COLS), f),
        pltpu.VMEM((bt, LRU_WIDTH), f),
    ]
    return pl.pallas_call(
        functools.partial(_layer_kernel, nb=nb, t_len=t_len),
        grid=grid,
        in_specs=in_specs,
        out_specs=out_specs,
        out_shape=out_shape,
        scratch_shapes=scratch,
        compiler_params=pltpu.CompilerParams(
            dimension_semantics=("arbitrary", "arbitrary"),
            vmem_limit_bytes=VMEM_LIMIT_BYTES),
        name=f"hybrid_layer_nb{nb}_t{t_len}",
    )(x, s5_0, lru_0, conv_0, *weights)


def _block_diag(blocks):
    n, r, c = blocks.shape[-3:]
    eye = jnp.eye(n, dtype=blocks.dtype)
    out = blocks[..., :, :, None, :] * eye[:, None, :, None]
    return out.reshape(*blocks.shape[:-3], n * r, n * c)


def _s5_state_to_cols(s):
    bsz = s.shape[0]
    s = s.reshape(bsz, N_QUADS, QUAD_GROUPS, S5_STATE, 2)
    return jnp.transpose(s, (0, 1, 4, 2, 3)).reshape(bsz, S5_COLS)


def _s5_cols_to_state(c):
    bsz = c.shape[0]
    c = c.reshape(bsz, N_QUADS, 2, QUAD_GROUPS, S5_STATE)
    return jnp.transpose(c, (0, 1, 3, 4, 2)).reshape(bsz, S5_GROUPS, S5_STATE, 2)


def kernel(x_prompt, x_sample, state_s5, state_lru, state_conv, ln_gain, w_in, s5_lambda_re, s5_lambda_im, s5_log_dt, s5_b_re, s5_b_im, s5_c_re, s5_c_im, s5_d, w_glu, b_glu, conv_w, conv_b, lru_wa, lru_ba, lru_wx, lru_bx, lru_lambda, w_pa, w_pb, w_out, final_gain):
    depth = ln_gain.shape[0]
    assert depth == 1
    l = 0
    f = jnp.float32
    lb_re, lb_im, bb_re, bb_im, c_imn, c_lru = _prep(
        s5_lambda_re[l], s5_lambda_im[l], s5_log_dt[l], s5_b_re[l], s5_b_im[l],
        s5_c_im[l], lru_lambda[l])

    def quads(a):
        return a.reshape(N_QUADS, QUAD_GROUPS, *a.shape[1:])
    lb = jnp.concatenate([lb_re.reshape(N_QUADS, 1, QUAD_STATE),
                          lb_im.reshape(N_QUADS, 1, QUAD_STATE)], axis=1).reshape(1, S5_COLS)
    bc = _bf16(jnp.concatenate([_block_diag(quads(bb_re)), _block_diag(quads(bb_im))], axis=-1))
    c_re_t = jnp.swapaxes(quads(s5_c_re[l]), -1, -2)
    c_im_t = jnp.swapaxes(quads(c_imn), -1, -2)
    cc = _bf16(jnp.concatenate([_block_diag(c_re_t), _block_diag(c_im_t)], axis=-2))
    wa = _bf16(_block_diag(lru_wa[l].reshape(GATE_TILES, HEADS_PER_TILE, LRU_HEAD_DIM, LRU_HEAD_DIM)))
    wx = _bf16(_block_diag(lru_wx[l].reshape(GATE_TILES, HEADS_PER_TILE, LRU_HEAD_DIM, LRU_HEAD_DIM)))
    weights = (
        ln_gain[l].reshape(1, D_MODEL).astype(f), _bf16(w_in[l]), lb, bc, cc,
        s5_d[l].reshape(1, S5_WIDTH).astype(f), _bf16(w_glu[l]), b_glu[l].reshape(1, S5_WIDTH),
        conv_w[l], conv_b[l].reshape(1, LRU_WIDTH), wa, lru_ba[l].reshape(1, LRU_WIDTH),
        wx, lru_bx[l].reshape(1, LRU_WIDTH), c_lru, _bf16(w_pa[l]), _bf16(w_pb[l]),
        _bf16(w_out[l]), final_gain.reshape(1, D_MODEL),
    )

    bp = x_prompt.shape[0]
    y_p, s5_p, lru_p, conv_p = _run_group(
        x_prompt, jnp.zeros((bp, S5_COLS), f), jnp.zeros((bp, LRU_WIDTH), f),
        jnp.zeros((CONV_W - 1, bp, LRU_WIDTH), f), weights, nb=1, t_len=32)
    y_s, s5_s, lru_s, conv_s = _run_group(
        x_sample, _s5_state_to_cols(state_s5[l].astype(f)), state_lru[l].astype(f),
        jnp.swapaxes(state_conv[l].astype(f), 0, 1), weights, nb=2, t_len=16)

    sdt = state_s5.dtype
    return (y_p, y_s,
            _s5_cols_to_state(s5_p)[None].astype(sdt), lru_p[None].astype(state_lru.dtype),
            jnp.swapaxes(conv_p, 0, 1)[None].astype(state_conv.dtype),
            _s5_cols_to_state(s5_s)[None].astype(sdt), lru_s[None].astype(state_lru.dtype),
            jnp.swapaxes(conv_s, 0, 1)[None].astype(state_conv.dtype))
```

```python
import functools
import math

import jax
import jax.numpy as jnp
from jax import lax
from jax.experimental import pallas as pl
from jax.experimental.pallas import tpu as pltpu

D_MODEL = 1024
S5_GROUP = 16
S5_GROUPS = 32
S5_WIDTH = S5_GROUP * S5_GROUPS
S5_STATE = 64
LRU_WIDTH = D_MODEL
LRU_HEADS = 16
LRU_HEAD_DIM = LRU_WIDTH // LRU_HEADS
LRU_C = 8.0
CONV_W = 4
EPS = 1e-6

SUBLANES = 8
LANES = 128
MXU_DIM = 256
N_QUADS = S5_WIDTH // LANES
QUAD_GROUPS = S5_GROUPS // N_QUADS
QUAD_STATE = QUAD_GROUPS * S5_STATE
S5_COLS = 2 * S5_GROUPS * S5_STATE
GATE_TILES = LRU_WIDTH // MXU_DIM
HEADS_PER_TILE = MXU_DIM // LRU_HEAD_DIM
N_SLABS = D_MODEL // LANES
VMEM_LIMIT_BYTES = 56 * 1024 * 1024

C_UA, C_ZA, C_UB, C_ZB, C_GA, C_GB = 0, 512, 1024, 2048, 3072, 4096


def _bf16(a):
    return a.astype(jnp.bfloat16)


def _dot(a, b):
    return jnp.dot(a, b, preferred_element_type=jnp.float32)


def _prep_kernel(lam_re_ref, lam_im_ref, log_dt_ref, b_re_ref, b_im_ref, c_im_ref,
                 lru_lam_ref, lb_re_ref, lb_im_ref, bb_re_ref, bb_im_ref,
                 c_imn_ref, c_lru_ref):
    lam_re = lam_re_ref[...]
    lam_im = lam_im_ref[...]
    dt = jnp.exp(log_dt_ref[...])
    mag = jnp.exp(lam_re * dt)
    ang = lam_im * dt
    lb_re = mag * jnp.cos(ang)
    lb_im = mag * jnp.sin(ang)
    den = lam_re * lam_re + lam_im * lam_im
    nr = lb_re - 1.0
    coef_re = (nr * lam_re + lb_im * lam_im) / den
    coef_im = (lb_im * lam_re - nr * lam_im) / den
    b_re = b_re_ref[...]
    b_im = b_im_ref[...]
    lb_re_ref[...] = lb_re
    lb_im_ref[...] = lb_im
    bb_re_ref[...] = coef_re * b_re - coef_im * b_im
    bb_im_ref[...] = coef_re * b_im + coef_im * b_re
    c_imn_ref[...] = -c_im_ref[...]
    nl = -lru_lam_ref[...]
    sp = jnp.maximum(nl, 0.0) + jnp.log1p(jnp.exp(-jnp.abs(nl)))
    c_lru_ref[...] = -LRU_C * sp


def _prep(lam_re, lam_im, log_dt, b_re, b_im, c_im, lru_lam):
    g, p, h = S5_GROUPS, S5_STATE, S5_GROUP
    f = jnp.float32
    outs = pl.pallas_call(
        _prep_kernel,
        out_shape=(jax.ShapeDtypeStruct((g, 1, p), f), jax.ShapeDtypeStruct((g, 1, p), f),
                   jax.ShapeDtypeStruct((g, h, p), f), jax.ShapeDtypeStruct((g, h, p), f),
                   jax.ShapeDtypeStruct((g, h, p), f), jax.ShapeDtypeStruct((1, LRU_WIDTH), f)),
        name="s5_lru_param_prep",
    )(lam_re.reshape(g, 1, p), lam_im.reshape(g, 1, p), log_dt.reshape(g, 1, 1),
      jnp.swapaxes(b_re, 1, 2), jnp.swapaxes(b_im, 1, 2), c_im, lru_lam.reshape(1, LRU_WIDTH))
    return outs


def _layer_kernel(x_ref, s5_0_ref, lru_0_ref, conv_0_ref, lng_ref, w_in_ref, lb_ref,
                  bc_ref, cc_ref, d_ref, w_glu_ref, b_glu_ref, conv_w_ref, conv_b_ref,
                  wa_ref, ba_ref, wx_ref, bx_ref, c_lru_ref, w_pa_ref, w_pb_ref,
                  w_out_ref, fg_ref,
                  y_ref, s5_n_ref, lru_n_ref, conv_n_ref,
                  tr_ref, hb_ref, ws_ref, cv_ref, hs5_ref, hl_ref, *, nb, t_len):
    tile_rows = t_len * SUBLANES
    tail_rows = (CONV_W - 1) * SUBLANES
    k = pl.program_id(1)

    @pl.when(k == 0)
    def _init_state():
        hs5_ref[...] = s5_0_ref[...]
        hl_ref[...] = lru_0_ref[...]
        for j in range(nb):
            cv_ref[j, 0:tail_rows, :] = conv_0_ref[:, j * SUBLANES:(j + 1) * SUBLANES, :].reshape(
                tail_rows, LRU_WIDTH)

    for jb in range(nb * SUBLANES):
        j, b = divmod(jb, SUBLANES)
        xb = x_ref[jb]
        inv = lax.rsqrt(jnp.mean(xb * xb, axis=-1, keepdims=True) + EPS)
        hn = xb * inv * lng_ref[...]
        for c in range(N_SLABS):
            tr_ref[c, pl.ds(j * tile_rows + b, t_len, stride=SUBLANES), :] = (
                hn[:, c * LANES:(c + 1) * LANES])
    for c in range(N_SLABS):
        hb_ref[:, c * LANES:(c + 1) * LANES] = _bf16(tr_ref[c])
    hb = hb_ref[...]

    def proj(col, width):
        return _dot(hb, w_in_ref[:, col:col + width])

    ua = proj(C_UA, S5_WIDTH)
    ua_b = _bf16(ua)
    for q in range(N_QUADS):
        ws_ref[:, q * 2 * QUAD_STATE:(q + 1) * 2 * QUAD_STATE] = _dot(
            ua_b[:, q * LANES:(q + 1) * LANES], bc_ref[q])

    for j in range(nb):
        for q in range(N_QUADS):
            re0 = q * 2 * QUAD_STATE
            im0 = re0 + QUAD_STATE
            lbr = jnp.broadcast_to(lb_ref[:, re0:re0 + QUAD_STATE], (SUBLANES, QUAD_STATE))
            lbi = jnp.broadcast_to(lb_ref[:, im0:im0 + QUAD_STATE], (SUBLANES, QUAD_STATE))
            hr0 = hs5_ref[j * SUBLANES:(j + 1) * SUBLANES, re0:re0 + QUAD_STATE]
            hi0 = hs5_ref[j * SUBLANES:(j + 1) * SUBLANES, im0:im0 + QUAD_STATE]

            def s5_step(t, carry, j=j, re0=re0, im0=im0, lbr=lbr, lbi=lbi):
                hr, hi = carry
                row = pl.multiple_of(j * tile_rows + t * SUBLANES, SUBLANES)
                xr = ws_ref[pl.ds(row, SUBLANES), re0:re0 + QUAD_STATE]
                xi = ws_ref[pl.ds(row, SUBLANES), im0:im0 + QUAD_STATE]
                nr = lbr * hr - lbi * hi + xr
                ni = lbr * hi + lbi * hr + xi
                ws_ref[pl.ds(row, SUBLANES), re0:re0 + QUAD_STATE] = nr
                ws_ref[pl.ds(row, SUBLANES), im0:im0 + QUAD_STATE] = ni
                return nr, ni

            hr, hi = lax.fori_loop(0, t_len, s5_step, (hr0, hi0), unroll=True)
            hs5_ref[j * SUBLANES:(j + 1) * SUBLANES, re0:re0 + QUAD_STATE] = hr
            hs5_ref[j * SUBLANES:(j + 1) * SUBLANES, im0:im0 + QUAD_STATE] = hi
    s5_n_ref[...] = hs5_ref[...]

    ya_parts = []
    for q in range(N_QUADS):
        sq = _bf16(ws_ref[:, q * 2 * QUAD_STATE:(q + 1) * 2 * QUAD_STATE])
        ya_parts.append(_dot(sq, cc_ref[q]))
    ya = jnp.concatenate(ya_parts, axis=-1) + ua * d_ref[...]
    ya = jax.nn.gelu(ya)
    ya = ya * jax.nn.sigmoid(_dot(_bf16(ya), w_glu_ref[...]) + b_glu_ref[...])
    za = proj(C_ZA, S5_WIDTH)
    pa = _dot(_bf16(ya * (za * jax.nn.sigmoid(za))), w_pa_ref[...])
    ga = proj(C_GA, D_MODEL)
    m_cols = slice(2 * LRU_WIDTH, 3 * LRU_WIDTH)
    ws_ref[:, m_cols] = jax.nn.sigmoid(ga) * pa

    ub = proj(C_UB, LRU_WIDTH)
    a_cols = slice(0, LRU_WIDTH)
    g_cols = slice(LRU_WIDTH, 2 * LRU_WIDTH)
    for j in range(nb):
        cv_ref[j, tail_rows:tail_rows + tile_rows, :] = ub[j * tile_rows:(j + 1) * tile_rows]
        xc = conv_b_ref[...] + conv_w_ref[0:1, :] * cv_ref[j, 0:tile_rows, :]
        for kk in range(1, CONV_W):
            xc = xc + conv_w_ref[kk:kk + 1, :] * cv_ref[j, kk * SUBLANES:kk * SUBLANES + tile_rows, :]
        new_tail = cv_ref[j, tile_rows:tile_rows + tail_rows, :]
        cv_ref[j, 0:tail_rows, :] = new_tail
        conv_n_ref[:, j * SUBLANES:(j + 1) * SUBLANES, :] = new_tail.reshape(
            CONV_W - 1, SUBLANES, LRU_WIDTH)
        xc_b = _bf16(xc)
        for gt in range(GATE_TILES):
            cs = slice(gt * MXU_DIM, (gt + 1) * MXU_DIM)
            r = jax.nn.sigmoid(_dot(xc_b[:, cs], wa_ref[gt]) + ba_ref[:, cs])
            i = jax.nn.sigmoid(_dot(xc_b[:, cs], wx_ref[gt]) + bx_ref[:, cs])
            log_a = c_lru_ref[:, cs] * r
            th = jnp.tanh(log_a)
            gain = jnp.sqrt(jnp.maximum(-2.0 * th / (1.0 - th), 0.0))
            rows = slice(j * tile_rows, (j + 1) * tile_rows)
            ws_ref[rows, gt * MXU_DIM:(gt + 1) * MXU_DIM] = jnp.exp(log_a)
            ws_ref[rows, LRU_WIDTH + gt * MXU_DIM:LRU_WIDTH + (gt + 1) * MXU_DIM] = (
                gain * (i * xc[:, cs]))

        def lru_step(t, h, j=j):
            row = pl.multiple_of(j * tile_rows + t * SUBLANES, SUBLANES)
            h = ws_ref[pl.ds(row, SUBLANES), a_cols] * h + ws_ref[pl.ds(row, SUBLANES), g_cols]
            ws_ref[pl.ds(row, SUBLANES), g_cols] = h
            return h

        hl = lax.fori_loop(0, t_len, lru_step, hl_ref[j * SUBLANES:(j + 1) * SUBLANES, :],
                           unroll=True)
        hl_ref[j * SUBLANES:(j + 1) * SUBLANES, :] = hl
    lru_n_ref[...] = hl_ref[...]

    zb = proj(C_ZB, LRU_WIDTH)
    pb = _dot(_bf16(ws_ref[:, g_cols] * (zb * jax.nn.sigmoid(zb))), w_pb_ref[...])
    gb = proj(C_GB, D_MODEL)
    m = ws_ref[:, m_cols] + jax.nn.sigmoid(gb) * pb
    delta = _dot(_bf16(m), w_out_ref[...])

    for c in range(N_SLABS):
        tr_ref[c] = delta[:, c * LANES:(c + 1) * LANES]
    for jb in range(nb * SUBLANES):
        j, b = divmod(jb, SUBLANES)
        parts = [tr_ref[c, pl.ds(j * tile_rows + b, t_len, stride=SUBLANES), :]
                 for c in range(N_SLABS)]
        o = x_ref[jb] + jnp.concatenate(parts, axis=-1)
        inv = lax.rsqrt(jnp.mean(o * o, axis=-1, keepdims=True) + EPS)
        y_ref[jb] = o * inv * fg_ref[...]


def _full_spec(arr):
    zeros = (0,) * arr.ndim
    return pl.BlockSpec(arr.shape, lambda i, k, zeros=zeros: zeros)


def _run_group(x, s5_0, lru_0, conv_0, weights, *, nb, t_len):
    bsz, seq, _ = x.shape
    bt = nb * SUBLANES
    assert bsz % bt == 0 and seq % t_len == 0 and t_len % SUBLANES == 0
    rows = bt * t_len
    grid = (bsz // bt, seq // t_len)
    f = jnp.float32
    in_specs = [
        pl.BlockSpec((bt, t_len, D_MODEL), lambda i, k: (i, k, 0)),
        pl.BlockSpec((bt, S5_COLS), lambda i, k: (i, 0)),
        pl.BlockSpec((bt, LRU_WIDTH), lambda i, k: (i, 0)),
        pl.BlockSpec((CONV_W - 1, bt, LRU_WIDTH), lambda i, k: (0, i, 0)),
    ] + [_full_spec(w) for w in weights]
    out_specs = (
        pl.BlockSpec((bt, t_len, D_MODEL), lambda i, k: (i, k, 0)),
        pl.BlockSpec((bt, S5_COLS), lambda i, k: (i, 0)),
        pl.BlockSpec((bt, LRU_WIDTH), lambda i, k: (i, 0)),
        pl.BlockSpec((CONV_W - 1, bt, LRU_WIDTH), lambda i, k: (0, i, 0)),
    )
    out_shape = (
        jax.ShapeDtypeStruct((bsz, seq, D_MODEL), x.dtype),
        jax.ShapeDtypeStruct((bsz, S5_COLS), f),
        jax.ShapeDtypeStruct((bsz, LRU_WIDTH), f),
        jax.ShapeDtypeStruct((CONV_W - 1, bsz, LRU_WIDTH), f),
    )
    scratch = [
        pltpu.VMEM((N_SLABS, rows, LANES), f),
        pltpu.VMEM((rows, D_MODEL), jnp.bfloat16),
        pltpu.VMEM((rows, S5_COLS), f),
        pltpu.VMEM((nb, (t_len + CONV_W - 1) * SUBLANES, LRU_WIDTH), f),
        pltpu.VMEM((bt, S5_COLS), f),
        pltpu.VMEM((bt, LRU_WIDTH), f),
    ]
    return pl.pallas_call(
        functools.partial(_layer_kernel, nb=nb, t_len=t_len),
        grid=grid,
        in_specs=in_specs,
        out_specs=out_specs,
        out_shape=out_shape,
        scratch_shapes=scratch,
        compiler_params=pltpu.CompilerParams(
            dimension_semantics=("arbitrary", "arbitrary"),
            vmem_limit_bytes=VMEM_LIMIT_BYTES),
        name=f"hybrid_layer_nb{nb}_t{t_len}",
    )(x, s5_0, lru_0, conv_0, *weights)


def _block_diag(blocks):
    n, r, c = blocks.shape[-3:]
    eye = jnp.eye(n, dtype=blocks.dtype)
    out = blocks[..., :, :, None, :] * eye[:, None, :, None]
    return out.reshape(*blocks.shape[:-3], n * r, n * c)


def _s5_state_to_cols(s):
    bsz = s.shape[0]
    s = s.reshape(bsz, N_QUADS, QUAD_GROUPS, S5_STATE, 2)
    return jnp.transpose(s, (0, 1, 4, 2, 3)).reshape(bsz, S5_COLS)


def _s5_cols_to_state(c):
    bsz = c.shape[0]
    c = c.reshape(bsz, N_QUADS, 2, QUAD_GROUPS, S5_STATE)
    return jnp.transpose(c, (0, 1, 3, 4, 2)).reshape(bsz, S5_GROUPS, S5_STATE, 2)


def kernel(x_prompt, x_sample, state_s5, state_lru, state_conv, ln_gain, w_in, s5_lambda_re, s5_lambda_im, s5_log_dt, s5_b_re, s5_b_im, s5_c_re, s5_c_im, s5_d, w_glu, b_glu, conv_w, conv_b, lru_wa, lru_ba, lru_wx, lru_bx, lru_lambda, w_pa, w_pb, w_out, final_gain):
    depth = ln_gain.shape[0]
    assert depth == 1
    l = 0
    f = jnp.float32
    lb_re, lb_im, bb_re, bb_im, c_imn, c_lru = _prep(
        s5_lambda_re[l], s5_lambda_im[l], s5_log_dt[l], s5_b_re[l], s5_b_im[l],
        s5_c_im[l], lru_lambda[l])

    def quads(a):
        return a.reshape(N_QUADS, QUAD_GROUPS, *a.shape[1:])
    lb = jnp.concatenate([lb_re.reshape(N_QUADS, 1, QUAD_STATE),
                          lb_im.reshape(N_QUADS, 1, QUAD_STATE)], axis=1).reshape(1, S5_COLS)
    bc = _bf16(jnp.concatenate([_block_diag(quads(bb_re)), _block_diag(quads(bb_im))], axis=-1))
    c_re_t = jnp.swapaxes(quads(s5_c_re[l]), -1, -2)
    c_im_t = jnp.swapaxes(quads(c_imn), -1, -2)
    cc = _bf16(jnp.concatenate([_block_diag(c_re_t), _block_diag(c_im_t)], axis=-2))
    wa = _bf16(_block_diag(lru_wa[l].reshape(GATE_TILES, HEADS_PER_TILE, LRU_HEAD_DIM, LRU_HEAD_DIM)))
    wx = _bf16(_block_diag(lru_wx[l].reshape(GATE_TILES, HEADS_PER_TILE, LRU_HEAD_DIM, LRU_HEAD_DIM)))
    weights = (
        ln_gain[l].reshape(1, D_MODEL).astype(f), _bf16(w_in[l]), lb, bc, cc,
        s5_d[l].reshape(1, S5_WIDTH).astype(f), _bf16(w_glu[l]), b_glu[l].reshape(1, S5_WIDTH),
        conv_w[l], conv_b[l].reshape(1, LRU_WIDTH), wa, lru_ba[l].reshape(1, LRU_WIDTH),
        wx, lru_bx[l].reshape(1, LRU_WIDTH), c_lru, _bf16(w_pa[l]), _bf16(w_pb[l]),
        _bf16(w_out[l]), final_gain.reshape(1, D_MODEL),
    )

    bp = x_prompt.shape[0]
    y_p, s5_p, lru_p, conv_p = _run_group(
        x_prompt, jnp.zeros((bp, S5_COLS), f), jnp.zeros((bp, LRU_WIDTH), f),
        jnp.zeros((CONV_W - 1, bp, LRU_WIDTH), f), weights, nb=1, t_len=32)
    y_s, s5_s, lru_s, conv_s = _run_group(
        x_sample, _s5_state_to_cols(state_s5[l].astype(f)), state_lru[l].astype(f),
        jnp.swapaxes(state_conv[l].astype(f), 0, 1), weights, nb=2, t_len=16)

    sdt = state_s5.dtype
    return (y_p, y_s,
            _s5_cols_to_state(s5_p)[None].astype(sdt), lru_p[None].astype(state_lru.dtype),
            jnp.swapaxes(conv_p, 0, 1)[None].astype(state_conv.dtype),
            _s5_cols_to_state(s5_s)[None].astype(sdt), lru_s[None].astype(state_lru.dtype),
            jnp.swapaxes(conv_s, 0, 1)[None].astype(state_conv.dtype))
```

```python
import functools
import math

import jax
import jax.numpy as jnp
from jax import lax
from jax.experimental import pallas as pl
from jax.experimental.pallas import tpu as pltpu

D_MODEL = 1024
S5_GROUP = 16
S5_GROUPS = 32
S5_WIDTH = S5_GROUP * S5_GROUPS
S5_STATE = 64
LRU_WIDTH = D_MODEL
LRU_HEADS = 16
LRU_HEAD_DIM = LRU_WIDTH // LRU_HEADS
LRU_C = 8.0
CONV_W = 4
EPS = 1e-6

SUBLANES = 8
LANES = 128
MXU_DIM = 256
N_QUADS = S5_WIDTH // LANES
QUAD_GROUPS = S5_GROUPS // N_QUADS
QUAD_STATE = QUAD_GROUPS * S5_STATE
S5_COLS = 2 * S5_GROUPS * S5_STATE
GATE_TILES = LRU_WIDTH // MXU_DIM
HEADS_PER_TILE = MXU_DIM // LRU_HEAD_DIM
N_SLABS = D_MODEL // LANES
W_CHUNK = 512
VMEM_LIMIT_BYTES = 60 * 1024 * 1024

C_UA, C_ZA, C_UB, C_ZB, C_GA, C_GB = 0, 512, 1024, 2048, 3072, 4096


def _bf16(a):
    return a.astype(jnp.bfloat16)


def _dot(a, b):
    return jnp.dot(a, b, preferred_element_type=jnp.float32)


def _prep_kernel(lam_re_ref, lam_im_ref, log_dt_ref, b_re_ref, b_im_ref, c_im_ref,
                 lru_lam_ref, lb_re_ref, lb_im_ref, bb_re_ref, bb_im_ref,
                 c_imn_ref, c_lru_ref):
    lam_re = lam_re_ref[...]
    lam_im = lam_im_ref[...]
    dt = jnp.exp(log_dt_ref[...])
    mag = jnp.exp(lam_re * dt)
    ang = lam_im * dt
    lb_re = mag * jnp.cos(ang)
    lb_im = mag * jnp.sin(ang)
    den = lam_re * lam_re + lam_im * lam_im
    nr = lb_re - 1.0
    coef_re = (nr * lam_re + lb_im * lam_im) / den
    coef_im = (lb_im * lam_re - nr * lam_im) / den
    b_re = b_re_ref[...]
    b_im = b_im_ref[...]
    lb_re_ref[...] = lb_re
    lb_im_ref[...] = lb_im
    bb_re_ref[...] = coef_re * b_re - coef_im * b_im
    bb_im_ref[...] = coef_re * b_im + coef_im * b_re
    c_imn_ref[...] = -c_im_ref[...]
    nl = -lru_lam_ref[...]
    sp = jnp.maximum(nl, 0.0) + jnp.log1p(jnp.exp(-jnp.abs(nl)))
    c_lru_ref[...] = -LRU_C * sp


def _prep(lam_re, lam_im, log_dt, b_re, b_im, c_im, lru_lam):
    g, p, h = S5_GROUPS, S5_STATE, S5_GROUP
    f = jnp.float32
    outs = pl.pallas_call(
        _prep_kernel,
        out_shape=(jax.ShapeDtypeStruct((g, 1, p), f), jax.ShapeDtypeStruct((g, 1, p), f),
                   jax.ShapeDtypeStruct((g, h, p), f), jax.ShapeDtypeStruct((g, h, p), f),
                   jax.ShapeDtypeStruct((g, h, p), f), jax.ShapeDtypeStruct((1, LRU_WIDTH), f)),
        name="s5_lru_param_prep",
    )(lam_re.reshape(g, 1, p), lam_im.reshape(g, 1, p), log_dt.reshape(g, 1, 1),
      jnp.swapaxes(b_re, 1, 2), jnp.swapaxes(b_im, 1, 2), c_im, lru_lam.reshape(1, LRU_WIDTH))
    return outs


def _sub_tile_segments(s, rows_sub, nb, t_len):
    tile_rows = t_len * SUBLANES
    r0 = s * rows_sub
    if rows_sub <= tile_rows:
        j, off = divmod(r0, tile_rows)
        t0 = off // SUBLANES
        return r0, [(j, t0, t0 + rows_sub // SUBLANES)]
    return r0, [(j, 0, t_len) for j in range(r0 // tile_rows, (r0 + rows_sub) // tile_rows)]


def _layer_kernel(x_ref, s5_0_ref, lru_0_ref, conv_0_ref, lng_ref, w_in_ref, lb_ref,
                  bc_ref, cc_ref, d_ref, w_glu_ref, b_glu_ref, conv_w_ref, conv_b_ref,
                  wa_ref, ba_ref, wx_ref, bx_ref, c_lru_ref, w_pa_ref, w_pb_ref,
                  w_out_ref, fg_ref,
                  y_ref, s5_n_ref, lru_n_ref, conv_n_ref,
                  tr_ref, hb_ref, ws_ref, cv_ref, hs5_ref, hl_ref, *, nb, t_len, n_sub):
    tile_rows = t_len * SUBLANES
    tail_rows = (CONV_W - 1) * SUBLANES
    rows_sub = nb * tile_rows // n_sub
    a_cols = slice(0, LRU_WIDTH)
    g_cols = slice(LRU_WIDTH, 2 * LRU_WIDTH)
    m_cols = slice(2 * LRU_WIDTH, 3 * LRU_WIDTH)
    k = pl.program_id(1)

    @pl.when(k == 0)
    def _init_state():
        hs5_ref[...] = s5_0_ref[...]
        hl_ref[...] = lru_0_ref[...]
        for j in range(nb):
            cv_ref[j, 0:tail_rows, :] = conv_0_ref[:, j * SUBLANES:(j + 1) * SUBLANES, :].reshape(
                tail_rows, LRU_WIDTH)

    def sub(s):
        r0, segs = _sub_tile_segments(s, rows_sub, nb, t_len)
        return slice(r0, r0 + rows_sub), segs

    def wide_dot(lhs, w_ref, c0, n_chunks):
        parts = [_dot(lhs, w_ref[c0 + c]) for c in range(n_chunks)]
        return parts[0] if n_chunks == 1 else jnp.concatenate(parts, axis=-1)

    def proj(rows, col, width):
        return wide_dot(hb_ref[rows, :], w_in_ref, col // W_CHUNK, width // W_CHUNK)

    def st_norm_in(s, st):
        rows, segs = sub(s)
        for j, t0, t1 in segs:
            for b in range(SUBLANES):
                xb = x_ref[j * SUBLANES + b, t0:t1, :]
                inv = lax.rsqrt(jnp.mean(xb * xb, axis=-1, keepdims=True) + EPS)
                hn = xb * inv * lng_ref[...]
                for c in range(N_SLABS):
                    tr_ref[c, pl.ds((j * t_len + t0) * SUBLANES + b, t1 - t0, stride=SUBLANES), :] = (
                        hn[:, c * LANES:(c + 1) * LANES])
        for c in range(N_SLABS):
            hb_ref[rows, c * LANES:(c + 1) * LANES] = _bf16(tr_ref[c, rows, :])

    def st_proj_a(s, st):
        rows, _ = sub(s)
        pz = proj(rows, C_UA, 2 * S5_WIDTH)
        st["ua"] = pz[:, :S5_WIDTH]
        za = pz[:, S5_WIDTH:]
        st["sza"] = za * jax.nn.sigmoid(za)

    def st_s5_in(s, st):
        rows, _ = sub(s)
        ua_b = _bf16(st["ua"])
        for q in range(2 * N_QUADS):
            ws_ref[rows, q * QUAD_STATE:(q + 1) * QUAD_STATE] = _dot(
                ua_b[:, (q // 2) * LANES:(q // 2 + 1) * LANES], bc_ref[q])

    def st_scan_conv(s, st):
        rows, segs = sub(s)
        ub = proj(rows, C_UB, LRU_WIDTH)
        xcs = []
        off = 0
        for j, t0, t1 in segs:
            n = (t1 - t0) * SUBLANES
            cv_ref[j, tail_rows + t0 * SUBLANES:tail_rows + t1 * SUBLANES, :] = ub[off:off + n]
            off += n
        for j, t0, t1 in segs:
            base = j * tile_rows
            for q in range(N_QUADS):
                re0 = q * 2 * QUAD_STATE
                im0 = re0 + QUAD_STATE
                lbr = jnp.broadcast_to(lb_ref[:, re0:re0 + QUAD_STATE], (SUBLANES, QUAD_STATE))
                lbi = jnp.broadcast_to(lb_ref[:, im0:im0 + QUAD_STATE], (SUBLANES, QUAD_STATE))
                hr = hs5_ref[j * SUBLANES:(j + 1) * SUBLANES, re0:re0 + QUAD_STATE]
                hi = hs5_ref[j * SUBLANES:(j + 1) * SUBLANES, im0:im0 + QUAD_STATE]
                for t in range(t0, t1):
                    row = slice(base + t * SUBLANES, base + (t + 1) * SUBLANES)
                    xr = ws_ref[row, re0:re0 + QUAD_STATE]
                    xi = ws_ref[row, im0:im0 + QUAD_STATE]
                    hr, hi = lbr * hr - lbi * hi + xr, lbr * hi + lbi * hr + xi
                    ws_ref[row, re0:re0 + QUAD_STATE] = hr
                    ws_ref[row, im0:im0 + QUAD_STATE] = hi
                hs5_ref[j * SUBLANES:(j + 1) * SUBLANES, re0:re0 + QUAD_STATE] = hr
                hs5_ref[j * SUBLANES:(j + 1) * SUBLANES, im0:im0 + QUAD_STATE] = hi
        for j, t0, t1 in segs:
            n = (t1 - t0) * SUBLANES
            xc = conv_b_ref[...] + conv_w_ref[0:1, :] * cv_ref[j, t0 * SUBLANES:t0 * SUBLANES + n, :]
            for kk in range(1, CONV_W):
                lo = (t0 + kk) * SUBLANES
                xc = xc + conv_w_ref[kk:kk + 1, :] * cv_ref[j, lo:lo + n, :]
            xcs.append(xc)
        st["xc"] = xcs[0] if len(xcs) == 1 else jnp.concatenate(xcs, axis=0)

    def st_s5_out(s, st):
        rows, _ = sub(s)
        parts = []
        for q in range(N_QUADS):
            sq = _bf16(ws_ref[rows, q * 2 * QUAD_STATE:(q + 1) * 2 * QUAD_STATE])
            parts.append(_dot(sq, cc_ref[q]))
        ya = jnp.concatenate(parts, axis=-1) + st["ua"] * d_ref[...]
        st["ya"] = jax.nn.gelu(ya)

    def st_glu_gates(s, st):
        rows, _ = sub(s)
        ya = st["ya"]
        ya = ya * jax.nn.sigmoid(_dot(_bf16(ya), w_glu_ref[...]) + b_glu_ref[...])
        st["ya2"] = _bf16(ya * st["sza"])
        xc = st["xc"]
        xc_b = _bf16(xc)
        for gt in range(GATE_TILES):
            cs = slice(gt * MXU_DIM, (gt + 1) * MXU_DIM)
            r = jax.nn.sigmoid(_dot(xc_b[:, cs], wa_ref[gt]) + ba_ref[:, cs])
            i = jax.nn.sigmoid(_dot(xc_b[:, cs], wx_ref[gt]) + bx_ref[:, cs])
            log_a = c_lru_ref[:, cs] * r
            th = jnp.tanh(log_a)
            q = -2.0 * th / (1.0 - th)
            gain = jnp.where(q > 0.0, q * lax.rsqrt(q), 0.0)
            ws_ref[rows, gt * MXU_DIM:(gt + 1) * MXU_DIM] = jnp.exp(log_a)
            ws_ref[rows, LRU_WIDTH + gt * MXU_DIM:LRU_WIDTH + (gt + 1) * MXU_DIM] = (
                gain * (i * xc[:, cs]))

    def st_merge_a_lru(s, st):
        rows, segs = sub(s)
        pa = wide_dot(st["ya2"], w_pa_ref, 0, D_MODEL // W_CHUNK)
        ga = proj(rows, C_GA, D_MODEL)
        ws_ref[rows, m_cols] = jax.nn.sigmoid(ga) * pa
        for j, t0, t1 in segs:
            base = j * tile_rows
            h = hl_ref[j * SUBLANES:(j + 1) * SUBLANES, :]
            for t in range(t0, t1):
                row = slice(base + t * SUBLANES, base + (t + 1) * SUBLANES)
                h = ws_ref[row, a_cols] * h + ws_ref[row, g_cols]
                ws_ref[row, g_cols] = h
            hl_ref[j * SUBLANES:(j + 1) * SUBLANES, :] = h

    def st_gate_b(s, st):
        rows, _ = sub(s)
        zb = proj(rows, C_ZB, LRU_WIDTH)
        st["hz"] = _bf16(ws_ref[rows, g_cols] * (zb * jax.nn.sigmoid(zb)))

    def st_merge_b(s, st):
        rows, _ = sub(s)
        pb = wide_dot(st["hz"], w_pb_ref, 0, D_MODEL // W_CHUNK)
        gb = proj(rows, C_GB, D_MODEL)
        st["m"] = _bf16(ws_ref[rows, m_cols] + jax.nn.sigmoid(gb) * pb)

    def st_out(s, st):
        rows, _ = sub(s)
        delta = wide_dot(st["m"], w_out_ref, 0, D_MODEL // W_CHUNK)
        for c in range(N_SLABS):
            tr_ref[c, rows, :] = delta[:, c * LANES:(c + 1) * LANES]

    def st_norm_out(s, st):
        _, segs = sub(s)
        for j, t0, t1 in segs:
            for b in range(SUBLANES):
                parts = [tr_ref[c, pl.ds((j * t_len + t0) * SUBLANES + b, t1 - t0, stride=SUBLANES), :]
                         for c in range(N_SLABS)]
                o = x_ref[j * SUBLANES + b, t0:t1, :] + jnp.concatenate(parts, axis=-1)
                inv = lax.rsqrt(jnp.mean(o * o, axis=-1, keepdims=True) + EPS)
                y_ref[j * SUBLANES + b, t0:t1, :] = o * inv * fg_ref[...]

    stages = (st_norm_in, st_proj_a, st_s5_in, st_scan_conv, st_s5_out, st_glu_gates,
              st_merge_a_lru, st_gate_b, st_merge_b, st_out, st_norm_out)
    state = [dict() for _ in range(n_sub)]
    for step in range(len(stages) + n_sub - 1):
        for s in range(n_sub):
            if 0 <= step - s < len(stages):
                stages[step - s](s, state[s])

    s5_n_ref[...] = hs5_ref[...]
    lru_n_ref[...] = hl_ref[...]
    for j in range(nb):
        new_tail = cv_ref[j, tile_rows:tile_rows + tail_rows, :]
        cv_ref[j, 0:tail_rows, :] = new_tail
        conv_n_ref[:, j * SUBLANES:(j + 1) * SUBLANES, :] = new_tail.reshape(
            CONV_W - 1, SUBLANES, LRU_WIDTH)


def _full_spec(arr):
    zeros = (0,) * arr.ndim
    return pl.BlockSpec(arr.shape, lambda i, k, zeros=zeros: zeros)


def _run_group(x, s5_0, lru_0, conv_0, weights, *, nb, t_len, n_sub):
    bsz, seq, _ = x.shape
    bt = nb * SUBLANES
    assert bsz % bt == 0 and seq % t_len == 0 and t_len % SUBLANES == 0
    rows = bt * t_len
    grid = (bsz // bt, seq // t_len)
    f = jnp.float32
    in_specs = [
        pl.BlockSpec((bt, t_len, D_MODEL), lambda i, k: (i, k, 0)),
        pl.BlockSpec((bt, S5_COLS), lambda i, k: (i, 0)),
        pl.BlockSpec((bt, LRU_WIDTH), lambda i, k: (i, 0)),
        pl.BlockSpec((CONV_W - 1, bt, LRU_WIDTH), lambda i, k: (0, i, 0)),
    ] + [_full_spec(w) for w in weights]
    out_specs = (
        pl.BlockSpec((bt, t_len, D_MODEL), lambda i, k: (i, k, 0)),
        pl.BlockSpec((bt, S5_COLS), lambda i, k: (i, 0)),
        pl.BlockSpec((bt, LRU_WIDTH), lambda i, k: (i, 0)),
        pl.BlockSpec((CONV_W - 1, bt, LRU_WIDTH), lambda i, k: (0, i, 0)),
    )
    out_shape = (
        jax.ShapeDtypeStruct((bsz, seq, D_MODEL), x.dtype),
        jax.ShapeDtypeStruct((bsz, S5_COLS), f),
        jax.ShapeDtypeStruct((bsz, LRU_WIDTH), f),
        jax.ShapeDtypeStruct((CONV_W - 1, bsz, LRU_WIDTH), f),
    )
    scratch = [
        pltpu.VMEM((N_SLABS, rows, LANES), f),
        pltpu.VMEM((rows, D_MODEL), jnp.bfloat16),
        pltpu.VMEM((rows, S5_COLS), f),
        pltpu.VMEM((nb, (t_len + CONV_W - 1) * SUBLANES, LRU_WIDTH), f),
        pltpu.VMEM((bt, S5_COLS), f),
        pltpu.VMEM((bt, LRU_WIDTH), f),
    ]
    return pl.pallas_call(
        functools.partial(_layer_kernel, nb=nb, t_len=t_len, n_sub=n_sub),
        grid=grid,
        in_specs=in_specs,
        out_specs=out_specs,
        out_shape=out_shape,
        scratch_shapes=scratch,
        compiler_params=pltpu.CompilerParams(
            dimension_semantics=("arbitrary", "arbitrary"),
            vmem_limit_bytes=VMEM_LIMIT_BYTES),
        name=f"hybrid_layer_nb{nb}_t{t_len}",
    )(x, s5_0, lru_0, conv_0, *weights)


def _block_diag(blocks):
    n, r, c = blocks.shape[-3:]
    eye = jnp.eye(n, dtype=blocks.dtype)
    out = blocks[..., :, :, None, :] * eye[:, None, :, None]
    return out.reshape(*blocks.shape[:-3], n * r, n * c)


def _s5_state_to_cols(s):
    bsz = s.shape[0]
    s = s.reshape(bsz, N_QUADS, QUAD_GROUPS, S5_STATE, 2)
    return jnp.transpose(s, (0, 1, 4, 2, 3)).reshape(bsz, S5_COLS)


def _s5_cols_to_state(c):
    bsz = c.shape[0]
    c = c.reshape(bsz, N_QUADS, 2, QUAD_GROUPS, S5_STATE)
    return jnp.transpose(c, (0, 1, 3, 4, 2)).reshape(bsz, S5_GROUPS, S5_STATE, 2)


def kernel(x_prompt, x_sample, state_s5, state_lru, state_conv, ln_gain, w_in, s5_lambda_re, s5_lambda_im, s5_log_dt, s5_b_re, s5_b_im, s5_c_re, s5_c_im, s5_d, w_glu, b_glu, conv_w, conv_b, lru_wa, lru_ba, lru_wx, lru_bx, lru_lambda, w_pa, w_pb, w_out, final_gain):
    depth = ln_gain.shape[0]
    assert depth == 1
    l = 0
    f = jnp.float32
    lb_re, lb_im, bb_re, bb_im, c_imn, c_lru = _prep(
        s5_lambda_re[l], s5_lambda_im[l], s5_log_dt[l], s5_b_re[l], s5_b_im[l],
        s5_c_im[l], lru_lambda[l])

    def quads(a):
        return a.reshape(N_QUADS, QUAD_GROUPS, *a.shape[1:])
    lb = jnp.concatenate([lb_re.reshape(N_QUADS, 1, QUAD_STATE),
                          lb_im.reshape(N_QUADS, 1, QUAD_STATE)], axis=1).reshape(1, S5_COLS)
    bc = _bf16(jnp.stack([_block_diag(quads(bb_re)), _block_diag(quads(bb_im))], axis=1)).reshape(
        2 * N_QUADS, LANES, QUAD_STATE)

    def chunked(w):
        kk, nn = w.shape
        return jnp.swapaxes(_bf16(w).reshape(kk, nn // W_CHUNK, W_CHUNK), 0, 1)
    c_re_t = jnp.swapaxes(quads(s5_c_re[l]), -1, -2)
    c_im_t = jnp.swapaxes(quads(c_imn), -1, -2)
    cc = _bf16(jnp.concatenate([_block_diag(c_re_t), _block_diag(c_im_t)], axis=-2))
    wa = _bf16(_block_diag(lru_wa[l].reshape(GATE_TILES, HEADS_PER_TILE, LRU_HEAD_DIM, LRU_HEAD_DIM)))
    wx = _bf16(_block_diag(lru_wx[l].reshape(GATE_TILES, HEADS_PER_TILE, LRU_HEAD_DIM, LRU_HEAD_DIM)))
    weights = (
        ln_gain[l].reshape(1, D_MODEL).astype(f), chunked(w_in[l]), lb, bc, cc,
        s5_d[l].reshape(1, S5_WIDTH).astype(f), _bf16(w_glu[l]), b_glu[l].reshape(1, S5_WIDTH),
        conv_w[l], conv_b[l].reshape(1, LRU_WIDTH), wa, lru_ba[l].reshape(1, LRU_WIDTH),
        wx, lru_bx[l].reshape(1, LRU_WIDTH), c_lru, chunked(w_pa[l]), chunked(w_pb[l]),
        chunked(w_out[l]), final_gain.reshape(1, D_MODEL),
    )

    bp = x_prompt.shape[0]
    y_p, s5_p, lru_p, conv_p = _run_group(
        x_prompt, jnp.zeros((bp, S5_COLS), f), jnp.zeros((bp, LRU_WIDTH), f),
        jnp.zeros((CONV_W - 1, bp, LRU_WIDTH), f), weights, nb=1, t_len=64, n_sub=4)
    y_s, s5_s, lru_s, conv_s = _run_group(
        x_sample, _s5_state_to_cols(state_s5[l].astype(f)), state_lru[l].astype(f),
        jnp.swapaxes(state_conv[l].astype(f), 0, 1), weights, nb=4, t_len=16, n_sub=2)

    sdt = state_s5.dtype
    return (y_p, y_s,
            _s5_cols_to_state(s5_p)[None].astype(sdt), lru_p[None].astype(state_lru.dtype),
            jnp.swapaxes(conv_p, 0, 1)[None].astype(state_conv.dtype),
            _s5_cols_to_state(s5_s)[None].astype(sdt), lru_s[None].astype(state_lru.dtype),
            jnp.swapaxes(conv_s, 0, 1)[None].astype(state_conv.dtype))
```

```python
import functools
import math

import jax
import jax.numpy as jnp
from jax import lax
from jax.experimental import pallas as pl
from jax.experimental.pallas import tpu as pltpu

D_MODEL = 1024
S5_GROUP = 16
S5_GROUPS = 32
S5_WIDTH = S5_GROUP * S5_GROUPS
S5_STATE = 64
LRU_WIDTH = D_MODEL
LRU_HEADS = 16
LRU_HEAD_DIM = LRU_WIDTH // LRU_HEADS
LRU_C = 8.0
CONV_W = 4
EPS = 1e-6

SUBLANES = 8
LANES = 128
MXU_DIM = 256
N_QUADS = S5_WIDTH // LANES
QUAD_GROUPS = S5_GROUPS // N_QUADS
QUAD_STATE = QUAD_GROUPS * S5_STATE
S5_COLS = 2 * S5_GROUPS * S5_STATE
GATE_TILES = LRU_WIDTH // MXU_DIM
HEADS_PER_TILE = MXU_DIM // LRU_HEAD_DIM
N_SLABS = D_MODEL // LANES
W_CHUNK = 512
VMEM_LIMIT_BYTES = 60 * 1024 * 1024

C_UA, C_ZA, C_UB, C_ZB, C_GA, C_GB = 0, 512, 1024, 2048, 3072, 4096


def _bf16(a):
    return a.astype(jnp.bfloat16)


def _dot(a, b):
    return jnp.dot(a, b, preferred_element_type=jnp.float32)


def _prep_kernel(lam_re_ref, lam_im_ref, log_dt_ref, b_re_ref, b_im_ref, c_im_ref,
                 lru_lam_ref, lb_re_ref, lb_im_ref, bb_re_ref, bb_im_ref,
                 c_imn_ref, c_lru_ref):
    lam_re = lam_re_ref[...]
    lam_im = lam_im_ref[...]
    dt = jnp.exp(log_dt_ref[...])
    mag = jnp.exp(lam_re * dt)
    ang = lam_im * dt
    lb_re = mag * jnp.cos(ang)
    lb_im = mag * jnp.sin(ang)
    den = lam_re * lam_re + lam_im * lam_im
    nr = lb_re - 1.0
    coef_re = (nr * lam_re + lb_im * lam_im) / den
    coef_im = (lb_im * lam_re - nr * lam_im) / den
    b_re = b_re_ref[...]
    b_im = b_im_ref[...]
    lb_re_ref[...] = lb_re
    lb_im_ref[...] = lb_im
    bb_re_ref[...] = coef_re * b_re - coef_im * b_im
    bb_im_ref[...] = coef_re * b_im + coef_im * b_re
    c_imn_ref[...] = -c_im_ref[...]
    nl = -lru_lam_ref[...]
    sp = jnp.maximum(nl, 0.0) + jnp.log1p(jnp.exp(-jnp.abs(nl)))
    c_lru_ref[...] = -LRU_C * sp


def _prep(lam_re, lam_im, log_dt, b_re, b_im, c_im, lru_lam):
    g, p, h = S5_GROUPS, S5_STATE, S5_GROUP
    f = jnp.float32
    outs = pl.pallas_call(
        _prep_kernel,
        out_shape=(jax.ShapeDtypeStruct((g, 1, p), f), jax.ShapeDtypeStruct((g, 1, p), f),
                   jax.ShapeDtypeStruct((g, h, p), f), jax.ShapeDtypeStruct((g, h, p), f),
                   jax.ShapeDtypeStruct((g, h, p), f), jax.ShapeDtypeStruct((1, LRU_WIDTH), f)),
        name="s5_lru_param_prep",
    )(lam_re.reshape(g, 1, p), lam_im.reshape(g, 1, p), log_dt.reshape(g, 1, 1),
      jnp.swapaxes(b_re, 1, 2), jnp.swapaxes(b_im, 1, 2), c_im, lru_lam.reshape(1, LRU_WIDTH))
    return outs


def _sub_tile_segments(s, rows_sub, nb, t_len):
    tile_rows = t_len * SUBLANES
    r0 = s * rows_sub
    if rows_sub <= tile_rows:
        j, off = divmod(r0, tile_rows)
        t0 = off // SUBLANES
        return r0, [(j, t0, t0 + rows_sub // SUBLANES)]
    return r0, [(j, 0, t_len) for j in range(r0 // tile_rows, (r0 + rows_sub) // tile_rows)]


def _layer_kernel(x_ref, *refs, nb, t_len, n_sub, zero_state):
    if zero_state:
        s5_0_ref = lru_0_ref = conv_0_ref = None
    else:
        s5_0_ref, lru_0_ref, conv_0_ref = refs[:3]
        refs = refs[3:]
    (lng_ref, w_in_ref, lb_ref, bc_ref, cc_ref, d_ref, w_glu_ref, b_glu_ref, conv_w_ref,
     conv_b_ref, wa_ref, ba_ref, wx_ref, bx_ref, c_lru_ref, w_pa_ref, w_pb_ref, w_out_ref, fg_ref,
     y_ref, s5_n_ref, lru_n_ref, conv_n_ref,
     tr_ref, hb_ref, ws_ref, cv_ref, hs5_ref, hl_ref) = refs
    tile_rows = t_len * SUBLANES
    tail_rows = (CONV_W - 1) * SUBLANES
    rows_sub = nb * tile_rows // n_sub
    a_cols = slice(0, LRU_WIDTH)
    g_cols = slice(LRU_WIDTH, 2 * LRU_WIDTH)
    m_cols = slice(2 * LRU_WIDTH, 3 * LRU_WIDTH)
    k = pl.program_id(1)

    @pl.when(k == 0)
    def _init_state():
        if zero_state:
            hs5_ref[...] = jnp.zeros_like(hs5_ref)
            hl_ref[...] = jnp.zeros_like(hl_ref)
            cv_ref[:, 0:tail_rows, :] = jnp.zeros((nb, tail_rows, LRU_WIDTH), jnp.float32)
        else:
            hs5_ref[...] = s5_0_ref[...]
            hl_ref[...] = lru_0_ref[...]
            for j in range(nb):
                cv_ref[j, 0:tail_rows, :] = conv_0_ref[
                    :, j * SUBLANES:(j + 1) * SUBLANES, :].reshape(tail_rows, LRU_WIDTH)

    def sub(s):
        r0, segs = _sub_tile_segments(s, rows_sub, nb, t_len)
        return slice(r0, r0 + rows_sub), segs

    def wide_dot(lhs, w_ref, c0, n_chunks):
        parts = [_dot(lhs, w_ref[c0 + c]) for c in range(n_chunks)]
        return parts[0] if n_chunks == 1 else jnp.concatenate(parts, axis=-1)

    def proj(rows, col, width):
        return wide_dot(hb_ref[rows, :], w_in_ref, col // W_CHUNK, width // W_CHUNK)

    def st_norm_in(s, st):
        rows, segs = sub(s)
        for j, t0, t1 in segs:
            for b in range(SUBLANES):
                xb = x_ref[j * SUBLANES + b, t0:t1, :]
                inv = lax.rsqrt(jnp.mean(xb * xb, axis=-1, keepdims=True) + EPS)
                hn = xb * inv * lng_ref[...]
                for c in range(N_SLABS):
                    tr_ref[c, pl.ds((j * t_len + t0) * SUBLANES + b, t1 - t0, stride=SUBLANES), :] = (
                        hn[:, c * LANES:(c + 1) * LANES])
        for c in range(N_SLABS):
            hb_ref[rows, c * LANES:(c + 1) * LANES] = _bf16(tr_ref[c, rows, :])

    def st_proj_a(s, st):
        rows, _ = sub(s)
        pz = proj(rows, C_UA, 2 * S5_WIDTH)
        st["ua"] = pz[:, :S5_WIDTH]
        za = pz[:, S5_WIDTH:]
        st["sza"] = za * jax.nn.sigmoid(za)

    def st_s5_in(s, st):
        rows, _ = sub(s)
        ua_b = _bf16(st["ua"])
        for q in range(2 * N_QUADS):
            ws_ref[rows, q * QUAD_STATE:(q + 1) * QUAD_STATE] = _dot(
                ua_b[:, (q // 2) * LANES:(q // 2 + 1) * LANES], bc_ref[q])

    def st_scan_conv(s, st):
        rows, segs = sub(s)
        ub = proj(rows, C_UB, LRU_WIDTH)
        xcs = []
        off = 0
        for j, t0, t1 in segs:
            n = (t1 - t0) * SUBLANES
            cv_ref[j, tail_rows + t0 * SUBLANES:tail_rows + t1 * SUBLANES, :] = ub[off:off + n]
            off += n
        for j, t0, t1 in segs:
            base = j * tile_rows
            for q in range(N_QUADS):
                re0 = q * 2 * QUAD_STATE
                im0 = re0 + QUAD_STATE
                lbr = jnp.broadcast_to(lb_ref[:, re0:re0 + QUAD_STATE], (SUBLANES, QUAD_STATE))
                lbi = jnp.broadcast_to(lb_ref[:, im0:im0 + QUAD_STATE], (SUBLANES, QUAD_STATE))
                hr = hs5_ref[j * SUBLANES:(j + 1) * SUBLANES, re0:re0 + QUAD_STATE]
                hi = hs5_ref[j * SUBLANES:(j + 1) * SUBLANES, im0:im0 + QUAD_STATE]
                for t in range(t0, t1):
                    row = slice(base + t * SUBLANES, base + (t + 1) * SUBLANES)
                    xr = ws_ref[row, re0:re0 + QUAD_STATE]
                    xi = ws_ref[row, im0:im0 + QUAD_STATE]
                    hr, hi = lbr * hr - lbi * hi + xr, lbr * hi + lbi * hr + xi
                    ws_ref[row, re0:re0 + QUAD_STATE] = hr
                    ws_ref[row, im0:im0 + QUAD_STATE] = hi
                hs5_ref[j * SUBLANES:(j + 1) * SUBLANES, re0:re0 + QUAD_STATE] = hr
                hs5_ref[j * SUBLANES:(j + 1) * SUBLANES, im0:im0 + QUAD_STATE] = hi
        for j, t0, t1 in segs:
            n = (t1 - t0) * SUBLANES
            xc = conv_b_ref[...] + conv_w_ref[0:1, :] * cv_ref[j, t0 * SUBLANES:t0 * SUBLANES + n, :]
            for kk in range(1, CONV_W):
                lo = (t0 + kk) * SUBLANES
                xc = xc + conv_w_ref[kk:kk + 1, :] * cv_ref[j, lo:lo + n, :]
            xcs.append(xc)
        st["xc"] = xcs[0] if len(xcs) == 1 else jnp.concatenate(xcs, axis=0)

    def st_s5_out(s, st):
        rows, _ = sub(s)
        parts = []
        for q in range(N_QUADS):
            sq = _bf16(ws_ref[rows, q * 2 * QUAD_STATE:(q + 1) * 2 * QUAD_STATE])
            parts.append(_dot(sq, cc_ref[q]))
        ya = jnp.concatenate(parts, axis=-1) + st["ua"] * d_ref[...]
        st["ya"] = jax.nn.gelu(ya)

    def st_glu_gates(s, st):
        rows, _ = sub(s)
        ya = st["ya"]
        ya = ya * jax.nn.sigmoid(_dot(_bf16(ya), w_glu_ref[...]) + b_glu_ref[...])
        st["ya2"] = _bf16(ya * st["sza"])
        xc = st["xc"]
        xc_b = _bf16(xc)
        for gt in range(GATE_TILES):
            cs = slice(gt * MXU_DIM, (gt + 1) * MXU_DIM)
            r = jax.nn.sigmoid(_dot(xc_b[:, cs], wa_ref[gt]) + ba_ref[:, cs])
            i = jax.nn.sigmoid(_dot(xc_b[:, cs], wx_ref[gt]) + bx_ref[:, cs])
            log_a = c_lru_ref[:, cs] * r
            th = jnp.tanh(log_a)
            q = -2.0 * th / (1.0 - th)
            gain = jnp.where(q > 0.0, q * lax.rsqrt(q), 0.0)
            ws_ref[rows, gt * MXU_DIM:(gt + 1) * MXU_DIM] = jnp.exp(log_a)
            ws_ref[rows, LRU_WIDTH + gt * MXU_DIM:LRU_WIDTH + (gt + 1) * MXU_DIM] = (
                gain * (i * xc[:, cs]))

    def st_merge_a_lru(s, st):
        rows, segs = sub(s)
        pa = wide_dot(st["ya2"], w_pa_ref, 0, D_MODEL // W_CHUNK)
        ga = proj(rows, C_GA, D_MODEL)
        ws_ref[rows, m_cols] = jax.nn.sigmoid(ga) * pa
        for j, t0, t1 in segs:
            base = j * tile_rows
            h = hl_ref[j * SUBLANES:(j + 1) * SUBLANES, :]
            for t in range(t0, t1):
                row = slice(base + t * SUBLANES, base + (t + 1) * SUBLANES)
                h = ws_ref[row, a_cols] * h + ws_ref[row, g_cols]
                ws_ref[row, g_cols] = h
            hl_ref[j * SUBLANES:(j + 1) * SUBLANES, :] = h

    def st_gate_b(s, st):
        rows, _ = sub(s)
        zb = proj(rows, C_ZB, LRU_WIDTH)
        st["hz"] = _bf16(ws_ref[rows, g_cols] * (zb * jax.nn.sigmoid(zb)))

    def st_merge_b(s, st):
        rows, _ = sub(s)
        pb = wide_dot(st["hz"], w_pb_ref, 0, D_MODEL // W_CHUNK)
        gb = proj(rows, C_GB, D_MODEL)
        st["m"] = _bf16(ws_ref[rows, m_cols] + jax.nn.sigmoid(gb) * pb)

    def st_out(s, st):
        rows, _ = sub(s)
        delta = wide_dot(st["m"], w_out_ref, 0, D_MODEL // W_CHUNK)
        for c in range(N_SLABS):
            tr_ref[c, rows, :] = delta[:, c * LANES:(c + 1) * LANES]

    def st_norm_out(s, st):
        _, segs = sub(s)
        for j, t0, t1 in segs:
            for b in range(SUBLANES):
                parts = [tr_ref[c, pl.ds((j * t_len + t0) * SUBLANES + b, t1 - t0, stride=SUBLANES), :]
                         for c in range(N_SLABS)]
                o = x_ref[j * SUBLANES + b, t0:t1, :] + jnp.concatenate(parts, axis=-1)
                inv = lax.rsqrt(jnp.mean(o * o, axis=-1, keepdims=True) + EPS)
                y_ref[j * SUBLANES + b, t0:t1, :] = o * inv * fg_ref[...]

    stages = (st_norm_in, st_proj_a, st_s5_in, st_scan_conv, st_s5_out, st_glu_gates,
              st_merge_a_lru, st_gate_b, st_merge_b, st_out, st_norm_out)
    state = [dict() for _ in range(n_sub)]
    for step in range(len(stages) + n_sub - 1):
        for s in range(n_sub):
            if 0 <= step - s < len(stages):
                stages[step - s](s, state[s])

    s5_n_ref[...] = hs5_ref[...]
    lru_n_ref[...] = hl_ref[...]
    for j in range(nb):
        new_tail = cv_ref[j, tile_rows:tile_rows + tail_rows, :]
        cv_ref[j, 0:tail_rows, :] = new_tail
        conv_n_ref[:, j * SUBLANES:(j + 1) * SUBLANES, :] = new_tail.reshape(
            CONV_W - 1, SUBLANES, LRU_WIDTH)


def _full_spec(arr):
    zeros = (0,) * arr.ndim
    return pl.BlockSpec(arr.shape, lambda i, k, zeros=zeros: zeros)


def _run_group(x, state, weights, *, nb, t_len, n_sub):
    bsz, seq, _ = x.shape
    bt = nb * SUBLANES
    assert bsz % bt == 0 and seq % t_len == 0 and t_len % SUBLANES == 0
    rows = bt * t_len
    grid = (bsz // bt, seq // t_len)
    f = jnp.float32
    state_specs = [
        pl.BlockSpec((bt, S5_COLS), lambda i, k: (i, 0)),
        pl.BlockSpec((bt, LRU_WIDTH), lambda i, k: (i, 0)),
        pl.BlockSpec((CONV_W - 1, bt, LRU_WIDTH), lambda i, k: (0, i, 0)),
    ]
    zero_state = state is None
    state = () if zero_state else tuple(state)
    in_specs = ([pl.BlockSpec((bt, t_len, D_MODEL), lambda i, k: (i, k, 0))]
                + ([] if zero_state else state_specs) + [_full_spec(w) for w in weights])
    out_specs = (
        pl.BlockSpec((bt, t_len, D_MODEL), lambda i, k: (i, k, 0)),
        pl.BlockSpec((bt, S5_COLS), lambda i, k: (i, 0)),
        pl.BlockSpec((bt, LRU_WIDTH), lambda i, k: (i, 0)),
        pl.BlockSpec((CONV_W - 1, bt, LRU_WIDTH), lambda i, k: (0, i, 0)),
    )
    out_shape = (
        jax.ShapeDtypeStruct((bsz, seq, D_MODEL), x.dtype),
        jax.ShapeDtypeStruct((bsz, S5_COLS), f),
        jax.ShapeDtypeStruct((bsz, LRU_WIDTH), f),
        jax.ShapeDtypeStruct((CONV_W - 1, bsz, LRU_WIDTH), f),
    )
    scratch = [
        pltpu.VMEM((N_SLABS, rows, LANES), f),
        pltpu.VMEM((rows, D_MODEL), jnp.bfloat16),
        pltpu.VMEM((rows, S5_COLS), f),
        pltpu.VMEM((nb, (t_len + CONV_W - 1) * SUBLANES, LRU_WIDTH), f),
        pltpu.VMEM((bt, S5_COLS), f),
        pltpu.VMEM((bt, LRU_WIDTH), f),
    ]
    return pl.pallas_call(
        functools.partial(_layer_kernel, nb=nb, t_len=t_len, n_sub=n_sub, zero_state=zero_state),
        grid=grid,
        in_specs=in_specs,
        out_specs=out_specs,
        out_shape=out_shape,
        scratch_shapes=scratch,
        compiler_params=pltpu.CompilerParams(
            dimension_semantics=("arbitrary", "arbitrary"),
            vmem_limit_bytes=VMEM_LIMIT_BYTES),
        name=f"hybrid_layer_nb{nb}_t{t_len}",
    )(x, *state, *weights)


def _cast_kernel(w_in_ref, w_glu_ref, w_pa_ref, w_pb_ref, w_out_ref,
                 o_in_ref, o_glu_ref, o_pa_ref, o_pb_ref, o_out_ref):
    i = pl.program_id(0)
    o_in_ref[0] = _bf16(w_in_ref[...])

    @pl.when(i == 0)
    def _():
        o_glu_ref[...] = _bf16(w_glu_ref[...])

    @pl.when(i < D_MODEL // W_CHUNK)
    def _():
        o_pa_ref[0] = _bf16(w_pa_ref[...])
        o_pb_ref[0] = _bf16(w_pb_ref[...])
        o_out_ref[0] = _bf16(w_out_ref[...])


def _cast_weights(w_in, w_glu, w_pa, w_pb, w_out):
    n_in = w_in.shape[1] // W_CHUNK
    n_d = D_MODEL // W_CHUNK
    last = n_d - 1

    def col_spec(w, clamp):
        k = w.shape[0]
        if clamp:
            return pl.BlockSpec((k, W_CHUNK), lambda i: (0, jnp.minimum(i, last)))
        return pl.BlockSpec((k, W_CHUNK), lambda i: (0, i))

    def chunk_spec(w, clamp):
        k = w.shape[0]
        if clamp:
            return pl.BlockSpec((1, k, W_CHUNK), lambda i: (jnp.minimum(i, last), 0, 0))
        return pl.BlockSpec((1, k, W_CHUNK), lambda i: (i, 0, 0))

    def chunk_shape(w):
        return jax.ShapeDtypeStruct((w.shape[1] // W_CHUNK, w.shape[0], W_CHUNK), jnp.bfloat16)

    glu_spec = pl.BlockSpec(w_glu.shape, lambda i: (0, 0))
    return pl.pallas_call(
        _cast_kernel,
        grid=(n_in,),
        in_specs=[col_spec(w_in, False), glu_spec, col_spec(w_pa, True), col_spec(w_pb, True),
                  col_spec(w_out, True)],
        out_specs=(chunk_spec(w_in, False), glu_spec, chunk_spec(w_pa, True),
                   chunk_spec(w_pb, True), chunk_spec(w_out, True)),
        out_shape=(chunk_shape(w_in), jax.ShapeDtypeStruct(w_glu.shape, jnp.bfloat16),
                   chunk_shape(w_pa), chunk_shape(w_pb), chunk_shape(w_out)),
        compiler_params=pltpu.CompilerParams(dimension_semantics=("arbitrary",)),
        name="weights_to_bf16_chunks",
    )(w_in, w_glu, w_pa, w_pb, w_out)


def _block_diag(blocks):
    n, r, c = blocks.shape[-3:]
    eye = jnp.eye(n, dtype=blocks.dtype)
    out = blocks[..., :, :, None, :] * eye[:, None, :, None]
    return out.reshape(*blocks.shape[:-3], n * r, n * c)


def _s5_state_to_cols(s):
    bsz = s.shape[0]
    s = s.reshape(bsz, N_QUADS, QUAD_GROUPS, S5_STATE, 2)
    return jnp.transpose(s, (0, 1, 4, 2, 3)).reshape(bsz, S5_COLS)


def _s5_cols_to_state(c):
    bsz = c.shape[0]
    c = c.reshape(bsz, N_QUADS, 2, QUAD_GROUPS, S5_STATE)
    return jnp.transpose(c, (0, 1, 3, 4, 2)).reshape(bsz, S5_GROUPS, S5_STATE, 2)


def kernel(x_prompt, x_sample, state_s5, state_lru, state_conv, ln_gain, w_in, s5_lambda_re, s5_lambda_im, s5_log_dt, s5_b_re, s5_b_im, s5_c_re, s5_c_im, s5_d, w_glu, b_glu, conv_w, conv_b, lru_wa, lru_ba, lru_wx, lru_bx, lru_lambda, w_pa, w_pb, w_out, final_gain):
    depth = ln_gain.shape[0]
    assert depth == 1
    l = 0
    f = jnp.float32
    lb_re, lb_im, bb_re, bb_im, c_imn, c_lru = _prep(
        s5_lambda_re[l], s5_lambda_im[l], s5_log_dt[l], s5_b_re[l], s5_b_im[l],
        s5_c_im[l], lru_lambda[l])

    def quads(a):
        return a.reshape(N_QUADS, QUAD_GROUPS, *a.shape[1:])
    lb = jnp.concatenate([lb_re.reshape(N_QUADS, 1, QUAD_STATE),
                          lb_im.reshape(N_QUADS, 1, QUAD_STATE)], axis=1).reshape(1, S5_COLS)
    bc = _bf16(jnp.stack([_block_diag(quads(bb_re)), _block_diag(quads(bb_im))], axis=1)).reshape(
        2 * N_QUADS, LANES, QUAD_STATE)

    w_in_c, w_glu_c, w_pa_c, w_pb_c, w_out_c = _cast_weights(
        w_in[l].astype(f), w_glu[l].astype(f), w_pa[l].astype(f), w_pb[l].astype(f),
        w_out[l].astype(f))
    c_re_t = jnp.swapaxes(quads(s5_c_re[l]), -1, -2)
    c_im_t = jnp.swapaxes(quads(c_imn), -1, -2)
    cc = _bf16(jnp.concatenate([_block_diag(c_re_t), _block_diag(c_im_t)], axis=-2))
    wa = _bf16(_block_diag(lru_wa[l].reshape(GATE_TILES, HEADS_PER_TILE, LRU_HEAD_DIM, LRU_HEAD_DIM)))
    wx = _bf16(_block_diag(lru_wx[l].reshape(GATE_TILES, HEADS_PER_TILE, LRU_HEAD_DIM, LRU_HEAD_DIM)))
    weights = (
        ln_gain[l].reshape(1, D_MODEL).astype(f), w_in_c, lb, bc, cc,
        s5_d[l].reshape(1, S5_WIDTH).astype(f), w_glu_c, b_glu[l].reshape(1, S5_WIDTH),
        conv_w[l], conv_b[l].reshape(1, LRU_WIDTH), wa, lru_ba[l].reshape(1, LRU_WIDTH),
        wx, lru_bx[l].reshape(1, LRU_WIDTH), c_lru, w_pa_c, w_pb_c,
        w_out_c, final_gain.reshape(1, D_MODEL),
    )

    y_p, s5_p, lru_p, conv_p = _run_group(x_prompt, None, weights, nb=1, t_len=64, n_sub=2)
    sample_state = (_s5_state_to_cols(state_s5[l].astype(f)), state_lru[l].astype(f),
                    jnp.swapaxes(state_conv[l].astype(f), 0, 1))
    y_s, s5_s, lru_s, conv_s = _run_group(x_sample, sample_state, weights, nb=4, t_len=16, n_sub=2)

    sdt = state_s5.dtype
    return (y_p, y_s,
            _s5_cols_to_state(s5_p)[None].astype(sdt), lru_p[None].astype(state_lru.dtype),
            jnp.swapaxes(conv_p, 0, 1)[None].astype(state_conv.dtype),
            _s5_cols_to_state(s5_s)[None].astype(sdt), lru_s[None].astype(state_lru.dtype),
            jnp.swapaxes(conv_s, 0, 1)[None].astype(state_conv.dtype))
```

```python
import functools
import math

import jax
import jax.numpy as jnp
from jax import lax
from jax.experimental import pallas as pl
from jax.experimental.pallas import tpu as pltpu

D_MODEL = 1024
S5_GROUP = 16
S5_GROUPS = 32
S5_WIDTH = S5_GROUP * S5_GROUPS
S5_STATE = 64
LRU_WIDTH = D_MODEL
LRU_HEADS = 16
LRU_HEAD_DIM = LRU_WIDTH // LRU_HEADS
LRU_C = 8.0
CONV_W = 4
EPS = 1e-6

SUBLANES = 8
LANES = 128
MXU_DIM = 256
N_QUADS = S5_WIDTH // LANES
QUAD_GROUPS = S5_GROUPS // N_QUADS
QUAD_STATE = QUAD_GROUPS * S5_STATE
S5_COLS = 2 * S5_GROUPS * S5_STATE
GATE_TILES = LRU_WIDTH // MXU_DIM
HEADS_PER_TILE = MXU_DIM // LRU_HEAD_DIM
N_SLABS = D_MODEL // LANES
W_CHUNK = 512
VMEM_LIMIT_BYTES = 60 * 1024 * 1024

C_UA, C_ZA, C_UB, C_ZB, C_GA, C_GB = 0, 512, 1024, 2048, 3072, 4096


def _bf16(a):
    return a.astype(jnp.bfloat16)


def _dot(a, b):
    return jnp.dot(a, b, preferred_element_type=jnp.float32)


def _prep_kernel(lam_re_ref, lam_im_ref, log_dt_ref, b_re_ref, b_im_ref, c_im_ref,
                 lru_lam_ref, lb_re_ref, lb_im_ref, bb_re_ref, bb_im_ref,
                 c_imn_ref, c_lru_ref):
    lam_re = lam_re_ref[...]
    lam_im = lam_im_ref[...]
    dt = jnp.exp(log_dt_ref[...])
    mag = jnp.exp(lam_re * dt)
    ang = lam_im * dt
    lb_re = mag * jnp.cos(ang)
    lb_im = mag * jnp.sin(ang)
    den = lam_re * lam_re + lam_im * lam_im
    nr = lb_re - 1.0
    coef_re = (nr * lam_re + lb_im * lam_im) / den
    coef_im = (lb_im * lam_re - nr * lam_im) / den
    b_re = b_re_ref[...]
    b_im = b_im_ref[...]
    lb_re_ref[...] = lb_re
    lb_im_ref[...] = lb_im
    bb_re_ref[...] = coef_re * b_re - coef_im * b_im
    bb_im_ref[...] = coef_re * b_im + coef_im * b_re
    c_imn_ref[...] = -c_im_ref[...]
    nl = -lru_lam_ref[...]
    sp = jnp.maximum(nl, 0.0) + jnp.log1p(jnp.exp(-jnp.abs(nl)))
    c_lru_ref[...] = -LRU_C * sp


def _prep(lam_re, lam_im, log_dt, b_re, b_im, c_im, lru_lam):
    g, p, h = S5_GROUPS, S5_STATE, S5_GROUP
    f = jnp.float32
    outs = pl.pallas_call(
        _prep_kernel,
        out_shape=(jax.ShapeDtypeStruct((g, 1, p), f), jax.ShapeDtypeStruct((g, 1, p), f),
                   jax.ShapeDtypeStruct((g, h, p), f), jax.ShapeDtypeStruct((g, h, p), f),
                   jax.ShapeDtypeStruct((g, h, p), f), jax.ShapeDtypeStruct((1, LRU_WIDTH), f)),
        name="s5_lru_param_prep",
    )(lam_re.reshape(g, 1, p), lam_im.reshape(g, 1, p), log_dt.reshape(g, 1, 1),
      jnp.swapaxes(b_re, 1, 2), jnp.swapaxes(b_im, 1, 2), c_im, lru_lam.reshape(1, LRU_WIDTH))
    return outs


_ITEM_SEQ = ("nin", "pa_", "pub", "s5i", "pga", "pzb", "gtm", "scn", "gat", "s5o", "pgb", "glu",
             "mpa", "lsc", "mpb", "out", "nou")
_ITEM_SKEW = 7


def _emission_order(n_sub):
    order = []
    for step in range(len(_ITEM_SEQ) + _ITEM_SKEW * (n_sub - 1)):
        for s in range(n_sub):
            idx = step - s * _ITEM_SKEW
            if 0 <= idx < len(_ITEM_SEQ):
                order.append((_ITEM_SEQ[idx], s))
    return order


def _sub_tile_segments(s, rows_sub, nb, t_len):
    tile_rows = t_len * SUBLANES
    r0 = s * rows_sub
    if rows_sub <= tile_rows:
        j, off = divmod(r0, tile_rows)
        t0 = off // SUBLANES
        return r0, [(j, t0, t0 + rows_sub // SUBLANES)]
    return r0, [(j, 0, t_len) for j in range(r0 // tile_rows, (r0 + rows_sub) // tile_rows)]


def _layer_kernel(x_ref, *refs, nb, t_len, n_sub, zero_state):
    if zero_state:
        s5_0_ref = lru_0_ref = conv_0_ref = None
    else:
        s5_0_ref, lru_0_ref, conv_0_ref = refs[:3]
        refs = refs[3:]
    (lng_ref, w_in_ref, lb_ref, bc_ref, cc_ref, d_ref, w_glu_ref, b_glu_ref, conv_w_ref,
     conv_b_ref, wa_ref, ba_ref, wx_ref, bx_ref, c_lru_ref, w_pa_ref, w_pb_ref, w_out_ref, fg_ref,
     y_ref, s5_n_ref, lru_n_ref, conv_n_ref,
     tr_ref, hb_ref, ws_ref, gp_ref, cv_ref, hs5_ref, hl_ref) = refs
    tile_rows = t_len * SUBLANES
    tail_rows = (CONV_W - 1) * SUBLANES
    rows_sub = nb * tile_rows // n_sub
    g_cols = slice(LRU_WIDTH, 2 * LRU_WIDTH)
    k = pl.program_id(1)

    @pl.when(k == 0)
    def _init_state():
        if zero_state:
            hs5_ref[...] = jnp.zeros_like(hs5_ref)
            hl_ref[...] = jnp.zeros_like(hl_ref)
            cv_ref[:, 0:tail_rows, :] = jnp.zeros((nb, tail_rows, LRU_WIDTH), jnp.float32)
        else:
            hs5_ref[...] = s5_0_ref[...]
            hl_ref[...] = lru_0_ref[...]
            for j in range(nb):
                cv_ref[j, 0:tail_rows, :] = conv_0_ref[
                    :, j * SUBLANES:(j + 1) * SUBLANES, :].reshape(tail_rows, LRU_WIDTH)

    def sub(s):
        r0, segs = _sub_tile_segments(s, rows_sub, nb, t_len)
        return slice(r0, r0 + rows_sub), segs

    def wide_dot(lhs, w_ref, c0, n_chunks):
        parts = [_dot(lhs, w_ref[c0 + c]) for c in range(n_chunks)]
        return parts[0] if n_chunks == 1 else jnp.concatenate(parts, axis=-1)

    def proj(rows, col, width):
        return wide_dot(hb_ref[rows, :], w_in_ref, col // W_CHUNK, width // W_CHUNK)

    def it_norm_in(s, st):
        rows, segs = sub(s)
        for j, t0, t1 in segs:
            for b in range(SUBLANES):
                xb = x_ref[j * SUBLANES + b, t0:t1, :]
                inv = lax.rsqrt(jnp.mean(xb * xb, axis=-1, keepdims=True) + EPS)
                hn = xb * inv * lng_ref[...]
                for c in range(N_SLABS):
                    tr_ref[c, pl.ds((j * t_len + t0) * SUBLANES + b, t1 - t0, stride=SUBLANES), :] = (
                        hn[:, c * LANES:(c + 1) * LANES])
        for c in range(N_SLABS):
            hb_ref[rows, c * LANES:(c + 1) * LANES] = _bf16(tr_ref[c, rows, :])

    def it_proj_a(s, st):
        rows, _ = sub(s)
        pz = proj(rows, C_UA, 2 * S5_WIDTH)
        st["ua"] = pz[:, :S5_WIDTH]
        za = pz[:, S5_WIDTH:]
        st["sza"] = za * jax.nn.sigmoid(za)

    def it_conv(s, st):
        rows, segs = sub(s)
        ub = proj(rows, C_UB, LRU_WIDTH)
        xcs = []
        off = 0
        for j, t0, t1 in segs:
            n = (t1 - t0) * SUBLANES
            cv_ref[j, tail_rows + t0 * SUBLANES:tail_rows + t1 * SUBLANES, :] = ub[off:off + n]
            off += n
        for j, t0, t1 in segs:
            n = (t1 - t0) * SUBLANES
            xc = conv_b_ref[...] + conv_w_ref[0:1, :] * cv_ref[j, t0 * SUBLANES:t0 * SUBLANES + n, :]
            for kk in range(1, CONV_W):
                lo = (t0 + kk) * SUBLANES
                xc = xc + conv_w_ref[kk:kk + 1, :] * cv_ref[j, lo:lo + n, :]
            xcs.append(xc)
        st["xc"] = xcs[0] if len(xcs) == 1 else jnp.concatenate(xcs, axis=0)

    def it_s5_in(s, st):
        rows, _ = sub(s)
        ua_b = _bf16(st["ua"])
        for q in range(2 * N_QUADS):
            ws_ref[rows, q * QUAD_STATE:(q + 1) * QUAD_STATE] = _dot(
                ua_b[:, (q // 2) * LANES:(q // 2 + 1) * LANES], bc_ref[q])

    def it_s5_scan(s, st):
        _, segs = sub(s)
        for j, t0, t1 in segs:
            base = j * tile_rows
            for q in range(N_QUADS):
                re0 = q * 2 * QUAD_STATE
                im0 = re0 + QUAD_STATE
                lbr = jnp.broadcast_to(lb_ref[:, re0:re0 + QUAD_STATE], (SUBLANES, QUAD_STATE))
                lbi = jnp.broadcast_to(lb_ref[:, im0:im0 + QUAD_STATE], (SUBLANES, QUAD_STATE))
                hr = hs5_ref[j * SUBLANES:(j + 1) * SUBLANES, re0:re0 + QUAD_STATE]
                hi = hs5_ref[j * SUBLANES:(j + 1) * SUBLANES, im0:im0 + QUAD_STATE]
                for t in range(t0, t1):
                    row = slice(base + t * SUBLANES, base + (t + 1) * SUBLANES)
                    xr = ws_ref[row, re0:re0 + QUAD_STATE]
                    xi = ws_ref[row, im0:im0 + QUAD_STATE]
                    hr, hi = lbr * hr - lbi * hi + xr, lbr * hi + lbi * hr + xi
                    ws_ref[row, re0:re0 + QUAD_STATE] = hr
                    ws_ref[row, im0:im0 + QUAD_STATE] = hi
                hs5_ref[j * SUBLANES:(j + 1) * SUBLANES, re0:re0 + QUAD_STATE] = hr
                hs5_ref[j * SUBLANES:(j + 1) * SUBLANES, im0:im0 + QUAD_STATE] = hi

    def it_gate_dots(s, st):
        rows, _ = sub(s)
        xc_b = _bf16(st["xc"])
        for gt in range(GATE_TILES):
            cs = slice(gt * MXU_DIM, (gt + 1) * MXU_DIM)
            gp_ref[rows, cs] = _dot(xc_b[:, cs], wa_ref[gt])
            gp_ref[rows, LRU_WIDTH + gt * MXU_DIM:LRU_WIDTH + (gt + 1) * MXU_DIM] = _dot(
                xc_b[:, cs], wx_ref[gt])

    def it_gates(s, st):
        rows, _ = sub(s)
        xc = st["xc"]
        tr = jnp.tanh(0.5 * (gp_ref[rows, 0:LRU_WIDTH] + ba_ref[...]))
        ti = jnp.tanh(0.5 * (gp_ref[rows, LRU_WIDTH:2 * LRU_WIDTH] + bx_ref[...]))
        c4 = 0.25 * c_lru_ref[...]
        t = jnp.tanh(c4 + c4 * tr)
        ri = 1.0 / (1.0 - t)
        nt = -t
        root = jnp.where(nt > 0.0, nt * lax.rsqrt(nt), 0.0)
        st["a"] = (1.0 + t) * ri
        st["g"] = (root * ri) * (xc + xc * ti)

    def it_proj_ga(s, st):
        rows, _ = sub(s)
        st["sga"] = jax.nn.sigmoid(proj(rows, C_GA, D_MODEL))

    def it_proj_zb(s, st):
        rows, _ = sub(s)
        zb = proj(rows, C_ZB, LRU_WIDTH)
        st["szb"] = zb * jax.nn.sigmoid(zb)

    def it_proj_gb(s, st):
        rows, _ = sub(s)
        st["sgb"] = jax.nn.sigmoid(proj(rows, C_GB, D_MODEL))

    def it_s5_out(s, st):
        rows, _ = sub(s)
        parts = []
        for q in range(N_QUADS):
            sq = _bf16(ws_ref[rows, q * 2 * QUAD_STATE:(q + 1) * 2 * QUAD_STATE])
            parts.append(_dot(sq, cc_ref[q]))
        ya = jnp.concatenate(parts, axis=-1) + st["ua"] * d_ref[...]
        st["ya"] = jax.nn.gelu(ya)

    def it_glu(s, st):
        ya = st["ya"]
        ya = ya * jax.nn.sigmoid(_dot(_bf16(ya), w_glu_ref[...]) + b_glu_ref[...])
        st["ya2"] = _bf16(ya * st["sza"])

    def it_lru_scan(s, st):
        _, segs = sub(s)
        a, g = st["a"], st["g"]
        off = 0
        for j, t0, t1 in segs:
            base = j * tile_rows
            h = hl_ref[j * SUBLANES:(j + 1) * SUBLANES, :]
            for t in range(t0, t1):
                lo = off + (t - t0) * SUBLANES
                h = a[lo:lo + SUBLANES] * h + g[lo:lo + SUBLANES]
                ws_ref[base + t * SUBLANES:base + (t + 1) * SUBLANES, g_cols] = h
            off += (t1 - t0) * SUBLANES
            hl_ref[j * SUBLANES:(j + 1) * SUBLANES, :] = h

    def it_merge_a(s, st):
        pa = wide_dot(st["ya2"], w_pa_ref, 0, D_MODEL // W_CHUNK)
        st["m"] = st["sga"] * pa

    def it_merge_b(s, st):
        rows, _ = sub(s)
        hz = _bf16(ws_ref[rows, g_cols] * st["szb"])
        pb = wide_dot(hz, w_pb_ref, 0, D_MODEL // W_CHUNK)
        st["mb"] = _bf16(st["m"] + st["sgb"] * pb)

    def it_out(s, st):
        rows, _ = sub(s)
        delta = wide_dot(st["mb"], w_out_ref, 0, D_MODEL // W_CHUNK)
        for c in range(N_SLABS):
            tr_ref[c, rows, :] = delta[:, c * LANES:(c + 1) * LANES]

    def it_norm_out(s, st):
        _, segs = sub(s)
        for j, t0, t1 in segs:
            for b in range(SUBLANES):
                parts = [tr_ref[c, pl.ds((j * t_len + t0) * SUBLANES + b, t1 - t0, stride=SUBLANES), :]
                         for c in range(N_SLABS)]
                o = x_ref[j * SUBLANES + b, t0:t1, :] + jnp.concatenate(parts, axis=-1)
                inv = lax.rsqrt(jnp.mean(o * o, axis=-1, keepdims=True) + EPS)
                y_ref[j * SUBLANES + b, t0:t1, :] = o * inv * fg_ref[...]

    items = dict(nin=it_norm_in, pa_=it_proj_a, pub=it_conv, s5i=it_s5_in, scn=it_s5_scan,
                 gtm=it_gate_dots, gat=it_gates, pga=it_proj_ga, pzb=it_proj_zb, pgb=it_proj_gb,
                 s5o=it_s5_out, glu=it_glu, lsc=it_lru_scan, mpa=it_merge_a, mpb=it_merge_b,
                 out=it_out, nou=it_norm_out)
    state = [dict() for _ in range(n_sub)]
    for name, s in _emission_order(n_sub):
        items[name](s, state[s])

    s5_n_ref[...] = hs5_ref[...]
    lru_n_ref[...] = hl_ref[...]
    for j in range(nb):
        new_tail = cv_ref[j, tile_rows:tile_rows + tail_rows, :]
        cv_ref[j, 0:tail_rows, :] = new_tail
        conv_n_ref[:, j * SUBLANES:(j + 1) * SUBLANES, :] = new_tail.reshape(
            CONV_W - 1, SUBLANES, LRU_WIDTH)


def _full_spec(arr):
    zeros = (0,) * arr.ndim
    return pl.BlockSpec(arr.shape, lambda i, k, zeros=zeros: zeros)


def _run_group(x, state, weights, *, nb, t_len, n_sub):
    bsz, seq, _ = x.shape
    bt = nb * SUBLANES
    assert bsz % bt == 0 and seq % t_len == 0 and t_len % SUBLANES == 0
    rows = bt * t_len
    grid = (bsz // bt, seq // t_len)
    f = jnp.float32
    state_specs = [
        pl.BlockSpec((bt, S5_COLS), lambda i, k: (i, 0)),
        pl.BlockSpec((bt, LRU_WIDTH), lambda i, k: (i, 0)),
        pl.BlockSpec((CONV_W - 1, bt, LRU_WIDTH), lambda i, k: (0, i, 0)),
    ]
    zero_state = state is None
    state = () if zero_state else tuple(state)
    in_specs = ([pl.BlockSpec((bt, t_len, D_MODEL), lambda i, k: (i, k, 0))]
                + ([] if zero_state else state_specs) + [_full_spec(w) for w in weights])
    out_specs = (
        pl.BlockSpec((bt, t_len, D_MODEL), lambda i, k: (i, k, 0)),
        pl.BlockSpec((bt, S5_COLS), lambda i, k: (i, 0)),
        pl.BlockSpec((bt, LRU_WIDTH), lambda i, k: (i, 0)),
        pl.BlockSpec((CONV_W - 1, bt, LRU_WIDTH), lambda i, k: (0, i, 0)),
    )
    out_shape = (
        jax.ShapeDtypeStruct((bsz, seq, D_MODEL), x.dtype),
        jax.ShapeDtypeStruct((bsz, S5_COLS), f),
        jax.ShapeDtypeStruct((bsz, LRU_WIDTH), f),
        jax.ShapeDtypeStruct((CONV_W - 1, bsz, LRU_WIDTH), f),
    )
    scratch = [
        pltpu.VMEM((N_SLABS, rows, LANES), f),
        pltpu.VMEM((rows, D_MODEL), jnp.bfloat16),
        pltpu.VMEM((rows, S5_COLS), f),
        pltpu.VMEM((rows, 2 * LRU_WIDTH), f),
        pltpu.VMEM((nb, (t_len + CONV_W - 1) * SUBLANES, LRU_WIDTH), f),
        pltpu.VMEM((bt, S5_COLS), f),
        pltpu.VMEM((bt, LRU_WIDTH), f),
    ]
    return pl.pallas_call(
        functools.partial(_layer_kernel, nb=nb, t_len=t_len, n_sub=n_sub, zero_state=zero_state),
        grid=grid,
        in_specs=in_specs,
        out_specs=out_specs,
        out_shape=out_shape,
        scratch_shapes=scratch,
        compiler_params=pltpu.CompilerParams(
            dimension_semantics=("arbitrary", "arbitrary"),
            vmem_limit_bytes=VMEM_LIMIT_BYTES),
        name=f"hybrid_layer_nb{nb}_t{t_len}",
    )(x, *state, *weights)


def _cast_kernel(w_in_ref, w_glu_ref, w_pa_ref, w_pb_ref, w_out_ref,
                 o_in_ref, o_glu_ref, o_pa_ref, o_pb_ref, o_out_ref):
    def chunks(src_ref, dst_ref):
        for c in range(dst_ref.shape[0]):
            dst_ref[c] = _bf16(src_ref[:, c * W_CHUNK:(c + 1) * W_CHUNK])

    chunks(w_in_ref, o_in_ref)

    @pl.when(pl.program_id(0) == 0)
    def _():
        o_glu_ref[...] = _bf16(w_glu_ref[...])
        chunks(w_pa_ref, o_pa_ref)
        chunks(w_pb_ref, o_pb_ref)
        chunks(w_out_ref, o_out_ref)


def _cast_weights(w_in, w_glu, w_pa, w_pb, w_out):
    per_step = D_MODEL // W_CHUNK
    n_steps = w_in.shape[1] // D_MODEL

    def whole(shape):
        return pl.BlockSpec(shape, lambda i, n=len(shape): (0,) * n)

    def chunk_shape(w):
        return jax.ShapeDtypeStruct((w.shape[1] // W_CHUNK, w.shape[0], W_CHUNK), jnp.bfloat16)

    outs = (chunk_shape(w_in), jax.ShapeDtypeStruct(w_glu.shape, jnp.bfloat16),
            chunk_shape(w_pa), chunk_shape(w_pb), chunk_shape(w_out))
    return pl.pallas_call(
        _cast_kernel,
        grid=(n_steps,),
        in_specs=[pl.BlockSpec((w_in.shape[0], D_MODEL), lambda i: (0, i)), whole(w_glu.shape),
                  whole(w_pa.shape), whole(w_pb.shape), whole(w_out.shape)],
        out_specs=(pl.BlockSpec((per_step, w_in.shape[0], W_CHUNK), lambda i: (i, 0, 0)),
                   whole(outs[1].shape), whole(outs[2].shape), whole(outs[3].shape),
                   whole(outs[4].shape)),
        out_shape=outs,
        compiler_params=pltpu.CompilerParams(dimension_semantics=("arbitrary",),
                                             vmem_limit_bytes=VMEM_LIMIT_BYTES),
        name="weights_to_bf16_chunks",
    )(w_in, w_glu, w_pa, w_pb, w_out)


def _block_diag(blocks):
    n, r, c = blocks.shape[-3:]
    eye = jnp.eye(n, dtype=blocks.dtype)
    out = blocks[..., :, :, None, :] * eye[:, None, :, None]
    return out.reshape(*blocks.shape[:-3], n * r, n * c)


def _s5_state_to_cols(s):
    bsz = s.shape[0]
    s = s.reshape(bsz, N_QUADS, QUAD_GROUPS, S5_STATE, 2)
    return jnp.transpose(s, (0, 1, 4, 2, 3)).reshape(bsz, S5_COLS)


def _s5_cols_to_state(c):
    bsz = c.shape[0]
    c = c.reshape(bsz, N_QUADS, 2, QUAD_GROUPS, S5_STATE)
    return jnp.transpose(c, (0, 1, 3, 4, 2)).reshape(bsz, S5_GROUPS, S5_STATE, 2)


def kernel(x_prompt, x_sample, state_s5, state_lru, state_conv, ln_gain, w_in, s5_lambda_re, s5_lambda_im, s5_log_dt, s5_b_re, s5_b_im, s5_c_re, s5_c_im, s5_d, w_glu, b_glu, conv_w, conv_b, lru_wa, lru_ba, lru_wx, lru_bx, lru_lambda, w_pa, w_pb, w_out, final_gain):
    depth = ln_gain.shape[0]
    assert depth == 1
    l = 0
    f = jnp.float32
    lb_re, lb_im, bb_re, bb_im, c_imn, c_lru = _prep(
        s5_lambda_re[l], s5_lambda_im[l], s5_log_dt[l], s5_b_re[l], s5_b_im[l],
        s5_c_im[l], lru_lambda[l])

    def quads(a):
        return a.reshape(N_QUADS, QUAD_GROUPS, *a.shape[1:])
    lb = jnp.concatenate([lb_re.reshape(N_QUADS, 1, QUAD_STATE),
                          lb_im.reshape(N_QUADS, 1, QUAD_STATE)], axis=1).reshape(1, S5_COLS)
    bc = _bf16(jnp.stack([_block_diag(quads(bb_re)), _block_diag(quads(bb_im))], axis=1)).reshape(
        2 * N_QUADS, LANES, QUAD_STATE)

    w_in_c, w_glu_c, w_pa_c, w_pb_c, w_out_c = _cast_weights(
        w_in[l].astype(f), w_glu[l].astype(f), w_pa[l].astype(f), w_pb[l].astype(f),
        w_out[l].astype(f))
    c_re_t = jnp.swapaxes(quads(s5_c_re[l]), -1, -2)
    c_im_t = jnp.swapaxes(quads(c_imn), -1, -2)
    cc = _bf16(jnp.concatenate([_block_diag(c_re_t), _block_diag(c_im_t)], axis=-2))
    wa = _bf16(_block_diag(lru_wa[l].reshape(GATE_TILES, HEADS_PER_TILE, LRU_HEAD_DIM, LRU_HEAD_DIM)))
    wx = _bf16(_block_diag(lru_wx[l].reshape(GATE_TILES, HEADS_PER_TILE, LRU_HEAD_DIM, LRU_HEAD_DIM)))
    weights = (
        ln_gain[l].reshape(1, D_MODEL).astype(f), w_in_c, lb, bc, cc,
        s5_d[l].reshape(1, S5_WIDTH).astype(f), w_glu_c, b_glu[l].reshape(1, S5_WIDTH),
        conv_w[l], conv_b[l].reshape(1, LRU_WIDTH), wa, lru_ba[l].reshape(1, LRU_WIDTH),
        wx, lru_bx[l].reshape(1, LRU_WIDTH), c_lru, w_pa_c, w_pb_c,
        w_out_c, final_gain.reshape(1, D_MODEL),
    )

    y_p, s5_p, lru_p, conv_p = _run_group(x_prompt, None, weights, nb=1, t_len=64, n_sub=2)
    sample_state = (_s5_state_to_cols(state_s5[l].astype(f)), state_lru[l].astype(f),
                    jnp.swapaxes(state_conv[l].astype(f), 0, 1))
    y_s, s5_s, lru_s, conv_s = _run_group(x_sample, sample_state, weights, nb=4, t_len=16, n_sub=2)

    sdt = state_s5.dtype
    return (y_p, y_s,
            _s5_cols_to_state(s5_p)[None].astype(sdt), lru_p[None].astype(state_lru.dtype),
            jnp.swapaxes(conv_p, 0, 1)[None].astype(state_conv.dtype),
            _s5_cols_to_state(s5_s)[None].astype(sdt), lru_s[None].astype(state_lru.dtype),
            jnp.swapaxes(conv_s, 0, 1)[None].astype(state_conv.dtype))
```

```python
import functools
import math

import jax
import jax.numpy as jnp
from jax import lax
from jax.experimental import pallas as pl
from jax.experimental.pallas import tpu as pltpu

D_MODEL = 1024
S5_GROUP = 16
S5_GROUPS = 32
S5_WIDTH = S5_GROUP * S5_GROUPS
S5_STATE = 64
LRU_WIDTH = D_MODEL
LRU_HEADS = 16
LRU_HEAD_DIM = LRU_WIDTH // LRU_HEADS
LRU_C = 8.0
CONV_W = 4
EPS = 1e-6

SUBLANES = 8
LANES = 128
MXU_DIM = 256
N_QUADS = S5_WIDTH // LANES
QUAD_GROUPS = S5_GROUPS // N_QUADS
QUAD_STATE = QUAD_GROUPS * S5_STATE
S5_COLS = 2 * S5_GROUPS * S5_STATE
GATE_TILES = LRU_WIDTH // MXU_DIM
HEADS_PER_TILE = MXU_DIM // LRU_HEAD_DIM
N_SLABS = D_MODEL // LANES
W_CHUNK = 512
VMEM_LIMIT_BYTES = 60 * 1024 * 1024

C_UA, C_ZA, C_UB, C_ZB, C_GA, C_GB = 0, 512, 1024, 2048, 3072, 4096


def _bf16(a):
    return a.astype(jnp.bfloat16)


def _dot(a, b):
    return jnp.dot(a, b, preferred_element_type=jnp.float32)


def _log2(n):
    assert n & (n - 1) == 0
    return n.bit_length() - 1


def _iota2(shape):
    return (lax.broadcasted_iota(jnp.int32, shape, 0), lax.broadcasted_iota(jnp.int32, shape, 1))


def _block_diag_rows(x, n, cb):
    rows = x.shape[0]
    rb = rows // n
    r, c = _iota2((cb, n * cb))
    tile = jnp.where((c & (cb - 1)) == r, 1.0, 0.0).astype(jnp.bfloat16)
    wide = _dot(_bf16(x), tile)
    r, c = _iota2((rows, n * cb))
    return jnp.where((r >> _log2(rb)) == (c >> _log2(cb)), wide, 0.0)


def _block_diag_rows_t(x, n, cb):
    rows = x.shape[0]
    rb = rows // n
    r, c = _iota2((n * cb, cb))
    tile_t = jnp.where((r & (cb - 1)) == c, 1.0, 0.0).astype(jnp.bfloat16)
    tall = lax.dot_general(tile_t, _bf16(x), (((1,), (1,)), ((), ())),
                           preferred_element_type=jnp.float32)
    r, c = _iota2((n * cb, rows))
    return jnp.where((r >> _log2(cb)) == (c >> _log2(rb)), tall, 0.0)


def _prep_kernel(lam_re_ref, lam_im_ref, log_dt_ref, b_re_ref, b_im_ref, c_re_ref, c_im_ref,
                 d_ref, wa_ref, wx_ref, ba_ref, bx_ref, lru_lam_ref,
                 lb_ref, bc_ref, cc_ref, d_o_ref, wa_o_ref, wx_o_ref, ba_o_ref, bx_o_ref, c_lru_ref):
    lam_re = lam_re_ref[...]
    lam_im = lam_im_ref[...]
    dt = jnp.exp(log_dt_ref[...])
    mag = jnp.exp(lam_re * dt)
    ang = lam_im * dt
    lb_re = mag * jnp.cos(ang)
    lb_im = mag * jnp.sin(ang)
    den = lam_re * lam_re + lam_im * lam_im
    nr = lb_re - 1.0
    coef_re = (nr * lam_re + lb_im * lam_im) / den
    coef_im = (lb_im * lam_re - nr * lam_im) / den
    b_re = b_re_ref[...]
    b_im = b_im_ref[...]
    bb = (coef_re * b_re - coef_im * b_im, coef_re * b_im + coef_im * b_re)
    cm = (c_re_ref[...], -c_im_ref[...])
    lbs = (lb_re, lb_im)
    for q in range(N_QUADS):
        gs = slice(q * QUAD_GROUPS, (q + 1) * QUAD_GROUPS)
        for part in range(2):
            k = 2 * q + part
            lb_ref[:, k * QUAD_STATE:(k + 1) * QUAD_STATE] = jnp.concatenate(
                [lbs[part][g] for g in range(q * QUAD_GROUPS, (q + 1) * QUAD_GROUPS)], axis=-1)
            x = bb[part][gs].reshape(QUAD_GROUPS * S5_GROUP, S5_STATE)
            bc_ref[k] = _bf16(_block_diag_rows(x, QUAD_GROUPS, S5_STATE))
            y = cm[part][gs].reshape(QUAD_GROUPS * S5_GROUP, S5_STATE)
            cc_ref[q, part * QUAD_STATE:(part + 1) * QUAD_STATE, :] = _bf16(
                _block_diag_rows_t(y, QUAD_GROUPS, S5_STATE))
    d_o_ref[...] = jnp.concatenate([d_ref[g:g + 1, :] for g in range(S5_GROUPS)], axis=-1)
    for src, dst in ((wa_ref, wa_o_ref), (wx_ref, wx_o_ref)):
        w = src[...].reshape(LRU_HEADS * LRU_HEAD_DIM, LRU_HEAD_DIM)
        for gt in range(GATE_TILES):
            dst[gt] = _bf16(_block_diag_rows(w[gt * MXU_DIM:(gt + 1) * MXU_DIM], HEADS_PER_TILE,
                                             LRU_HEAD_DIM))
    for src, dst in ((ba_ref, ba_o_ref), (bx_ref, bx_o_ref)):
        dst[...] = jnp.concatenate([src[h:h + 1, :] for h in range(LRU_HEADS)], axis=-1)
    nl = -lru_lam_ref[...]
    sp = jnp.maximum(nl, 0.0) + jnp.log1p(jnp.exp(-jnp.abs(nl)))
    c_lru_ref[...] = -LRU_C * sp


def _prep(lam_re, lam_im, log_dt, b_re, b_im, c_re, c_im, d, wa, wx, ba, bx, lru_lam):
    g, p = S5_GROUPS, S5_STATE
    f = jnp.float32
    bf = jnp.bfloat16
    return pl.pallas_call(
        _prep_kernel,
        out_shape=(jax.ShapeDtypeStruct((1, S5_COLS), f),
                   jax.ShapeDtypeStruct((2 * N_QUADS, LANES, QUAD_STATE), bf),
                   jax.ShapeDtypeStruct((N_QUADS, 2 * QUAD_STATE, LANES), bf),
                   jax.ShapeDtypeStruct((1, S5_WIDTH), f),
                   jax.ShapeDtypeStruct((GATE_TILES, MXU_DIM, MXU_DIM), bf),
                   jax.ShapeDtypeStruct((GATE_TILES, MXU_DIM, MXU_DIM), bf),
                   jax.ShapeDtypeStruct((1, LRU_WIDTH), f), jax.ShapeDtypeStruct((1, LRU_WIDTH), f),
                   jax.ShapeDtypeStruct((1, LRU_WIDTH), f)),
        name="s5_lru_param_prep",
    )(lam_re.reshape(g, 1, p), lam_im.reshape(g, 1, p), log_dt.reshape(g, 1, 1),
      jnp.swapaxes(b_re, 1, 2), jnp.swapaxes(b_im, 1, 2), c_re, c_im, d, wa, wx, ba, bx,
      lru_lam.reshape(1, LRU_WIDTH))


_ITEM_SEQ = ("nin", "pa_", "pub", "s5i", "pga", "pzb", "gtm", "scn", "gat", "s5o", "pgb", "glu",
             "mpa", "lsc", "mpb", "out", "nou")
_ITEM_SKEW = 7


def _emission_order(n_sub):
    order = []
    for step in range(len(_ITEM_SEQ) + _ITEM_SKEW * (n_sub - 1)):
        for s in range(n_sub):
            idx = step - s * _ITEM_SKEW
            if 0 <= idx < len(_ITEM_SEQ):
                order.append((_ITEM_SEQ[idx], s))
    return order


def _sub_tile_segments(s, rows_sub, nb, t_len):
    tile_rows = t_len * SUBLANES
    r0 = s * rows_sub
    if rows_sub <= tile_rows:
        j, off = divmod(r0, tile_rows)
        t0 = off // SUBLANES
        return r0, [(j, t0, t0 + rows_sub // SUBLANES)]
    return r0, [(j, 0, t_len) for j in range(r0 // tile_rows, (r0 + rows_sub) // tile_rows)]


def _layer_kernel(x_ref, *refs, nb, t_len, n_sub, zero_state):
    if zero_state:
        s5_0_ref = lru_0_ref = conv_0_ref = None
    else:
        s5_0_ref, lru_0_ref, conv_0_ref = refs[:3]
        refs = refs[3:]
    (lng_ref, w_in_ref, lb_ref, bc_ref, cc_ref, d_ref, w_glu_ref, b_glu_ref, conv_w_ref,
     conv_b_ref, wa_ref, ba_ref, wx_ref, bx_ref, c_lru_ref, w_pa_ref, w_pb_ref, w_out_ref, fg_ref,
     y_ref, s5_n_ref, lru_n_ref, conv_n_ref,
     tr_ref, hb_ref, ws_ref, gp_ref, cv_ref, hs5_ref, hl_ref) = refs
    tile_rows = t_len * SUBLANES
    tail_rows = (CONV_W - 1) * SUBLANES
    rows_sub = nb * tile_rows // n_sub
    g_cols = slice(LRU_WIDTH, 2 * LRU_WIDTH)
    k = pl.program_id(1)

    @pl.when(k == 0)
    def _init_state():
        if zero_state:
            hs5_ref[...] = jnp.zeros_like(hs5_ref)
            hl_ref[...] = jnp.zeros_like(hl_ref)
            cv_ref[:, 0:tail_rows, :] = jnp.zeros((nb, tail_rows, LRU_WIDTH), jnp.float32)
        else:
            hs5_ref[...] = s5_0_ref[...]
            hl_ref[...] = lru_0_ref[...]
            for j in range(nb):
                cv_ref[j, 0:tail_rows, :] = conv_0_ref[
                    :, j * SUBLANES:(j + 1) * SUBLANES, :].reshape(tail_rows, LRU_WIDTH)

    def sub(s):
        r0, segs = _sub_tile_segments(s, rows_sub, nb, t_len)
        return slice(r0, r0 + rows_sub), segs

    def wide_dot(lhs, w_ref, c0, n_chunks):
        parts = [_dot(lhs, w_ref[c0 + c]) for c in range(n_chunks)]
        return parts[0] if n_chunks == 1 else jnp.concatenate(parts, axis=-1)

    def proj(rows, col, width):
        return wide_dot(hb_ref[rows, :], w_in_ref, col // W_CHUNK, width // W_CHUNK)

    def it_norm_in(s, st):
        rows, segs = sub(s)
        for j, t0, t1 in segs:
            for b in range(SUBLANES):
                xb = x_ref[j * SUBLANES + b, t0:t1, :]
                inv = lax.rsqrt(jnp.mean(xb * xb, axis=-1, keepdims=True) + EPS)
                hn = xb * inv * lng_ref[...]
                for c in range(N_SLABS):
                    tr_ref[c, pl.ds((j * t_len + t0) * SUBLANES + b, t1 - t0, stride=SUBLANES), :] = (
                        hn[:, c * LANES:(c + 1) * LANES])
        for c in range(N_SLABS):
            hb_ref[rows, c * LANES:(c + 1) * LANES] = _bf16(tr_ref[c, rows, :])

    def it_proj_a(s, st):
        rows, _ = sub(s)
        pz = proj(rows, C_UA, 2 * S5_WIDTH)
        st["ua"] = pz[:, :S5_WIDTH]
        za = pz[:, S5_WIDTH:]
        st["sza"] = za * jax.nn.sigmoid(za)

    def it_conv(s, st):
        rows, segs = sub(s)
        ub = proj(rows, C_UB, LRU_WIDTH)
        xcs = []
        off = 0
        for j, t0, t1 in segs:
            n = (t1 - t0) * SUBLANES
            cv_ref[j, tail_rows + t0 * SUBLANES:tail_rows + t1 * SUBLANES, :] = ub[off:off + n]
            off += n
        for j, t0, t1 in segs:
            n = (t1 - t0) * SUBLANES
            xc = conv_b_ref[...] + conv_w_ref[0:1, :] * cv_ref[j, t0 * SUBLANES:t0 * SUBLANES + n, :]
            for kk in range(1, CONV_W):
                lo = (t0 + kk) * SUBLANES
                xc = xc + conv_w_ref[kk:kk + 1, :] * cv_ref[j, lo:lo + n, :]
            xcs.append(xc)
        st["xc"] = xcs[0] if len(xcs) == 1 else jnp.concatenate(xcs, axis=0)

    def it_s5_in(s, st):
        rows, _ = sub(s)
        ua_b = _bf16(st["ua"])
        for q in range(2 * N_QUADS):
            ws_ref[rows, q * QUAD_STATE:(q + 1) * QUAD_STATE] = _dot(
                ua_b[:, (q // 2) * LANES:(q // 2 + 1) * LANES], bc_ref[q])

    def it_s5_scan(s, st):
        _, segs = sub(s)
        for j, t0, t1 in segs:
            base = j * tile_rows
            for q in range(N_QUADS):
                re0 = q * 2 * QUAD_STATE
                im0 = re0 + QUAD_STATE
                lbr = jnp.broadcast_to(lb_ref[:, re0:re0 + QUAD_STATE], (SUBLANES, QUAD_STATE))
                lbi = jnp.broadcast_to(lb_ref[:, im0:im0 + QUAD_STATE], (SUBLANES, QUAD_STATE))
                hr = hs5_ref[j * SUBLANES:(j + 1) * SUBLANES, re0:re0 + QUAD_STATE]
                hi = hs5_ref[j * SUBLANES:(j + 1) * SUBLANES, im0:im0 + QUAD_STATE]
                for t in range(t0, t1):
                    row = slice(base + t * SUBLANES, base + (t + 1) * SUBLANES)
                    xr = ws_ref[row, re0:re0 + QUAD_STATE]
                    xi = ws_ref[row, im0:im0 + QUAD_STATE]
                    hr, hi = lbr * hr - lbi * hi + xr, lbr * hi + lbi * hr + xi
                    ws_ref[row, re0:re0 + QUAD_STATE] = hr
                    ws_ref[row, im0:im0 + QUAD_STATE] = hi
                hs5_ref[j * SUBLANES:(j + 1) * SUBLANES, re0:re0 + QUAD_STATE] = hr
                hs5_ref[j * SUBLANES:(j + 1) * SUBLANES, im0:im0 + QUAD_STATE] = hi

    def it_gate_dots(s, st):
        rows, _ = sub(s)
        xc_b = _bf16(st["xc"])
        for gt in range(GATE_TILES):
            cs = slice(gt * MXU_DIM, (gt + 1) * MXU_DIM)
            gp_ref[rows, cs] = _dot(xc_b[:, cs], wa_ref[gt])
            gp_ref[rows, LRU_WIDTH + gt * MXU_DIM:LRU_WIDTH + (gt + 1) * MXU_DIM] = _dot(
                xc_b[:, cs], wx_ref[gt])

    def it_gates(s, st):
        rows, _ = sub(s)
        xc = st["xc"]
        tr = jnp.tanh(0.5 * (gp_ref[rows, 0:LRU_WIDTH] + ba_ref[...]))
        ti = jnp.tanh(0.5 * (gp_ref[rows, LRU_WIDTH:2 * LRU_WIDTH] + bx_ref[...]))
        c4 = 0.25 * c_lru_ref[...]
        t = jnp.tanh(c4 + c4 * tr)
        ri = 1.0 / (1.0 - t)
        nt = -t
        root = jnp.where(nt > 0.0, nt * lax.rsqrt(nt), 0.0)
        st["a"] = (1.0 + t) * ri
        st["g"] = (root * ri) * (xc + xc * ti)

    def it_proj_ga(s, st):
        rows, _ = sub(s)
        st["sga"] = jax.nn.sigmoid(proj(rows, C_GA, D_MODEL))

    def it_proj_zb(s, st):
        rows, _ = sub(s)
        zb = proj(rows, C_ZB, LRU_WIDTH)
        st["szb"] = zb * jax.nn.sigmoid(zb)

    def it_proj_gb(s, st):
        rows, _ = sub(s)
        st["sgb"] = jax.nn.sigmoid(proj(rows, C_GB, D_MODEL))

    def it_s5_out(s, st):
        rows, _ = sub(s)
        parts = []
        for q in range(N_QUADS):
            sq = _bf16(ws_ref[rows, q * 2 * QUAD_STATE:(q + 1) * 2 * QUAD_STATE])
            parts.append(_dot(sq, cc_ref[q]))
        ya = jnp.concatenate(parts, axis=-1) + st["ua"] * d_ref[...]
        st["ya"] = jax.nn.gelu(ya)

    def it_glu(s, st):
        ya = st["ya"]
        ya = ya * jax.nn.sigmoid(_dot(_bf16(ya), w_glu_ref[...]) + b_glu_ref[...])
        st["ya2"] = _bf16(ya * st["sza"])

    def it_lru_scan(s, st):
        _, segs = sub(s)
        a, g = st["a"], st["g"]
        off = 0
        for j, t0, t1 in segs:
            base = j * tile_rows
            h = hl_ref[j * SUBLANES:(j + 1) * SUBLANES, :]
            for t in range(t0, t1):
                lo = off + (t - t0) * SUBLANES
                h = a[lo:lo + SUBLANES] * h + g[lo:lo + SUBLANES]
                ws_ref[base + t * SUBLANES:base + (t + 1) * SUBLANES, g_cols] = h
            off += (t1 - t0) * SUBLANES
            hl_ref[j * SUBLANES:(j + 1) * SUBLANES, :] = h

    def it_merge_a(s, st):
        pa = wide_dot(st["ya2"], w_pa_ref, 0, D_MODEL // W_CHUNK)
        st["m"] = st["sga"] * pa

    def it_merge_b(s, st):
        rows, _ = sub(s)
        hz = _bf16(ws_ref[rows, g_cols] * st["szb"])
        pb = wide_dot(hz, w_pb_ref, 0, D_MODEL // W_CHUNK)
        st["mb"] = _bf16(st["m"] + st["sgb"] * pb)

    def it_out(s, st):
        rows, _ = sub(s)
        delta = wide_dot(st["mb"], w_out_ref, 0, D_MODEL // W_CHUNK)
        for c in range(N_SLABS):
            tr_ref[c, rows, :] = delta[:, c * LANES:(c + 1) * LANES]

    def it_norm_out(s, st):
        _, segs = sub(s)
        for j, t0, t1 in segs:
            for b in range(SUBLANES):
                parts = [tr_ref[c, pl.ds((j * t_len + t0) * SUBLANES + b, t1 - t0, stride=SUBLANES), :]
                         for c in range(N_SLABS)]
                o = x_ref[j * SUBLANES + b, t0:t1, :] + jnp.concatenate(parts, axis=-1)
                inv = lax.rsqrt(jnp.mean(o * o, axis=-1, keepdims=True) + EPS)
                y_ref[j * SUBLANES + b, t0:t1, :] = o * inv * fg_ref[...]

    items = dict(nin=it_norm_in, pa_=it_proj_a, pub=it_conv, s5i=it_s5_in, scn=it_s5_scan,
                 gtm=it_gate_dots, gat=it_gates, pga=it_proj_ga, pzb=it_proj_zb, pgb=it_proj_gb,
                 s5o=it_s5_out, glu=it_glu, lsc=it_lru_scan, mpa=it_merge_a, mpb=it_merge_b,
                 out=it_out, nou=it_norm_out)
    state = [dict() for _ in range(n_sub)]
    for name, s in _emission_order(n_sub):
        items[name](s, state[s])

    s5_n_ref[...] = hs5_ref[...]
    lru_n_ref[...] = hl_ref[...]
    for j in range(nb):
        new_tail = cv_ref[j, tile_rows:tile_rows + tail_rows, :]
        cv_ref[j, 0:tail_rows, :] = new_tail
        conv_n_ref[:, j * SUBLANES:(j + 1) * SUBLANES, :] = new_tail.reshape(
            CONV_W - 1, SUBLANES, LRU_WIDTH)


def _full_spec(arr):
    zeros = (0,) * arr.ndim
    return pl.BlockSpec(arr.shape, lambda i, k, zeros=zeros: zeros)


def _run_group(x, state, weights, *, nb, t_len, n_sub):
    bsz, seq, _ = x.shape
    bt = nb * SUBLANES
    assert bsz % bt == 0 and seq % t_len == 0 and t_len % SUBLANES == 0
    rows = bt * t_len
    grid = (bsz // bt, seq // t_len)
    f = jnp.float32
    state_specs = [
        pl.BlockSpec((bt, S5_COLS), lambda i, k: (i, 0)),
        pl.BlockSpec((bt, LRU_WIDTH), lambda i, k: (i, 0)),
        pl.BlockSpec((CONV_W - 1, bt, LRU_WIDTH), lambda i, k: (0, i, 0)),
    ]
    zero_state = state is None
    state = () if zero_state else tuple(state)
    in_specs = ([pl.BlockSpec((bt, t_len, D_MODEL), lambda i, k: (i, k, 0))]
                + ([] if zero_state else state_specs) + [_full_spec(w) for w in weights])
    out_specs = (
        pl.BlockSpec((bt, t_len, D_MODEL), lambda i, k: (i, k, 0)),
        pl.BlockSpec((bt, S5_COLS), lambda i, k: (i, 0)),
        pl.BlockSpec((bt, LRU_WIDTH), lambda i, k: (i, 0)),
        pl.BlockSpec((CONV_W - 1, bt, LRU_WIDTH), lambda i, k: (0, i, 0)),
    )
    out_shape = (
        jax.ShapeDtypeStruct((bsz, seq, D_MODEL), x.dtype),
        jax.ShapeDtypeStruct((bsz, S5_COLS), f),
        jax.ShapeDtypeStruct((bsz, LRU_WIDTH), f),
        jax.ShapeDtypeStruct((CONV_W - 1, bsz, LRU_WIDTH), f),
    )
    scratch = [
        pltpu.VMEM((N_SLABS, rows, LANES), f),
        pltpu.VMEM((rows, D_MODEL), jnp.bfloat16),
        pltpu.VMEM((rows, S5_COLS), f),
        pltpu.VMEM((rows, 2 * LRU_WIDTH), f),
        pltpu.VMEM((nb, (t_len + CONV_W - 1) * SUBLANES, LRU_WIDTH), f),
        pltpu.VMEM((bt, S5_COLS), f),
        pltpu.VMEM((bt, LRU_WIDTH), f),
    ]
    return pl.pallas_call(
        functools.partial(_layer_kernel, nb=nb, t_len=t_len, n_sub=n_sub, zero_state=zero_state),
        grid=grid,
        in_specs=in_specs,
        out_specs=out_specs,
        out_shape=out_shape,
        scratch_shapes=scratch,
        compiler_params=pltpu.CompilerParams(
            dimension_semantics=("arbitrary", "arbitrary"),
            vmem_limit_bytes=VMEM_LIMIT_BYTES),
        name=f"hybrid_layer_nb{nb}_t{t_len}",
    )(x, *state, *weights)


def _cast_kernel(w_in_ref, w_glu_ref, w_pa_ref, w_pb_ref, w_out_ref,
                 o_in_ref, o_glu_ref, o_pa_ref, o_pb_ref, o_out_ref):
    def chunks(src_ref, dst_ref):
        for c in range(dst_ref.shape[0]):
            dst_ref[c] = _bf16(src_ref[:, c * W_CHUNK:(c + 1) * W_CHUNK])

    chunks(w_in_ref, o_in_ref)

    @pl.when(pl.program_id(0) == 0)
    def _():
        o_glu_ref[...] = _bf16(w_glu_ref[...])
        chunks(w_pa_ref, o_pa_ref)
        chunks(w_pb_ref, o_pb_ref)
        chunks(w_out_ref, o_out_ref)


def _cast_weights(w_in, w_glu, w_pa, w_pb, w_out):
    per_step = D_MODEL // W_CHUNK
    n_steps = w_in.shape[1] // D_MODEL

    def whole(shape):
        return pl.BlockSpec(shape, lambda i, n=len(shape): (0,) * n)

    def chunk_shape(w):
        return jax.ShapeDtypeStruct((w.shape[1] // W_CHUNK, w.shape[0], W_CHUNK), jnp.bfloat16)

    outs = (chunk_shape(w_in), jax.ShapeDtypeStruct(w_glu.shape, jnp.bfloat16),
            chunk_shape(w_pa), chunk_shape(w_pb), chunk_shape(w_out))
    return pl.pallas_call(
        _cast_kernel,
        grid=(n_steps,),
        in_specs=[pl.BlockSpec((w_in.shape[0], D_MODEL), lambda i: (0, i)), whole(w_glu.shape),
                  whole(w_pa.shape), whole(w_pb.shape), whole(w_out.shape)],
        out_specs=(pl.BlockSpec((per_step, w_in.shape[0], W_CHUNK), lambda i: (i, 0, 0)),
                   whole(outs[1].shape), whole(outs[2].shape), whole(outs[3].shape),
                   whole(outs[4].shape)),
        out_shape=outs,
        compiler_params=pltpu.CompilerParams(dimension_semantics=("arbitrary",),
                                             vmem_limit_bytes=VMEM_LIMIT_BYTES),
        name="weights_to_bf16_chunks",
    )(w_in, w_glu, w_pa, w_pb, w_out)


def _s5_state_to_cols(s):
    bsz = s.shape[0]
    s = s.reshape(bsz, N_QUADS, QUAD_GROUPS, S5_STATE, 2)
    return jnp.transpose(s, (0, 1, 4, 2, 3)).reshape(bsz, S5_COLS)


def _s5_cols_to_state(c):
    bsz = c.shape[0]
    c = c.reshape(bsz, N_QUADS, 2, QUAD_GROUPS, S5_STATE)
    return jnp.transpose(c, (0, 1, 3, 4, 2)).reshape(bsz, S5_GROUPS, S5_STATE, 2)


def kernel(x_prompt, x_sample, state_s5, state_lru, state_conv, ln_gain, w_in, s5_lambda_re, s5_lambda_im, s5_log_dt, s5_b_re, s5_b_im, s5_c_re, s5_c_im, s5_d, w_glu, b_glu, conv_w, conv_b, lru_wa, lru_ba, lru_wx, lru_bx, lru_lambda, w_pa, w_pb, w_out, final_gain):
    depth = ln_gain.shape[0]
    assert depth == 1
    l = 0
    f = jnp.float32
    lb, bc, cc, d_row, wa, wx, ba_row, bx_row, c_lru = _prep(
        s5_lambda_re[l].astype(f), s5_lambda_im[l].astype(f), s5_log_dt[l].astype(f),
        s5_b_re[l].astype(f), s5_b_im[l].astype(f), s5_c_re[l].astype(f), s5_c_im[l].astype(f),
        s5_d[l].astype(f), lru_wa[l].astype(f), lru_wx[l].astype(f), lru_ba[l].astype(f),
        lru_bx[l].astype(f), lru_lambda[l].astype(f))
    w_in_c, w_glu_c, w_pa_c, w_pb_c, w_out_c = _cast_weights(
        w_in[l].astype(f), w_glu[l].astype(f), w_pa[l].astype(f), w_pb[l].astype(f),
        w_out[l].astype(f))
    weights = (
        ln_gain[l].reshape(1, D_MODEL).astype(f), w_in_c, lb, bc, cc, d_row, w_glu_c,
        b_glu[l].reshape(1, S5_WIDTH).astype(f), conv_w[l].astype(f),
        conv_b[l].reshape(1, LRU_WIDTH).astype(f), wa, ba_row, wx, bx_row, c_lru,
        w_pa_c, w_pb_c, w_out_c, final_gain.reshape(1, D_MODEL).astype(f),
    )

    y_p, s5_p, lru_p, conv_p = _run_group(x_prompt, None, weights, nb=1, t_len=64, n_sub=2)
    sample_state = (_s5_state_to_cols(state_s5[l].astype(f)), state_lru[l].astype(f),
                    jnp.swapaxes(state_conv[l].astype(f), 0, 1))
    y_s, s5_s, lru_s, conv_s = _run_group(x_sample, sample_state, weights, nb=4, t_len=16, n_sub=2)

    sdt = state_s5.dtype
    return (y_p, y_s,
            _s5_cols_to_state(s5_p)[None].astype(sdt), lru_p[None].astype(state_lru.dtype),
            jnp.swapaxes(conv_p, 0, 1)[None].astype(state_conv.dtype),
            _s5_cols_to_state(s5_s)[None].astype(sdt), lru_s[None].astype(state_lru.dtype),
            jnp.swapaxes(conv_s, 0, 1)[None].astype(state_conv.dtype))
```

```python
import functools

import jax
import jax.numpy as jnp
from jax import lax
from jax.experimental import pallas as pl
from jax.experimental.pallas import tpu as pltpu

D_MODEL = 1024
S5_GROUP = 16
S5_GROUPS = 32
S5_WIDTH = S5_GROUP * S5_GROUPS
S5_STATE = 64
LRU_WIDTH = D_MODEL
LRU_HEADS = 16
LRU_HEAD_DIM = LRU_WIDTH // LRU_HEADS
LRU_C = 8.0
CONV_W = 4
EPS = 1e-6

SUBLANES = 8
LANES = 128
MXU_DIM = 256
N_QUADS = S5_WIDTH // LANES
QUAD_GROUPS = S5_GROUPS // N_QUADS
QUAD_STATE = QUAD_GROUPS * S5_STATE
S5_COLS = 2 * S5_GROUPS * S5_STATE
GATE_TILES = LRU_WIDTH // MXU_DIM
HEADS_PER_TILE = MXU_DIM // LRU_HEAD_DIM
N_SLABS = D_MODEL // LANES
W_CHUNK = 512
STEP_ROWS = 512
SUB_TILE_ROWS = 256
V7X_VMEM_BYTES = 64 * 1024 * 1024
VMEM_LIMIT_BYTES = V7X_VMEM_BYTES - 4 * 1024 * 1024

C_UA, C_ZA, C_UB, C_ZB, C_GA, C_GB = 0, 512, 1024, 2048, 3072, 4096


def _bf16(a):
    return a.astype(jnp.bfloat16)


def _dot(a, b):
    return jnp.dot(a, b, preferred_element_type=jnp.float32)


def _sigmoid(x):
    return 0.5 * jnp.tanh(0.5 * x) + 0.5


def _silu(x):
    h = 0.5 * x
    return h + h * jnp.tanh(h)


def _log2(n):
    assert n & (n - 1) == 0
    return n.bit_length() - 1


def _iota2(shape):
    return (lax.broadcasted_iota(jnp.int32, shape, 0), lax.broadcasted_iota(jnp.int32, shape, 1))


def _block_diag_rows(x, n, cb):
    rows = x.shape[0]
    rb = rows // n
    r, c = _iota2((cb, n * cb))
    tile = jnp.where((c & (cb - 1)) == r, 1.0, 0.0).astype(jnp.bfloat16)
    wide = _dot(_bf16(x), tile)
    r, c = _iota2((rows, n * cb))
    return jnp.where((r >> _log2(rb)) == (c >> _log2(cb)), wide, 0.0)


def _block_diag_rows_t(x, n, cb):
    rows = x.shape[0]
    rb = rows // n
    r, c = _iota2((n * cb, cb))
    tile_t = jnp.where((r & (cb - 1)) == c, 1.0, 0.0).astype(jnp.bfloat16)
    tall = lax.dot_general(tile_t, _bf16(x), (((1,), (1,)), ((), ())),
                           preferred_element_type=jnp.float32)
    r, c = _iota2((n * cb, rows))
    return jnp.where((r >> _log2(cb)) == (c >> _log2(rb)), tall, 0.0)


def _prep_kernel(lam_re_ref, lam_im_ref, log_dt_ref, b_re_ref, b_im_ref, c_re_ref, c_im_ref,
                 d_ref, wa_ref, wx_ref, ba_ref, bx_ref, lru_lam_ref,
                 lb_ref, bc_ref, cc_ref, d_o_ref, wa_o_ref, wx_o_ref, ba_o_ref, bx_o_ref, c_lru_ref):
    lam_re = lam_re_ref[...]
    lam_im = lam_im_ref[...]
    dt = jnp.exp(log_dt_ref[...])
    mag = jnp.exp(lam_re * dt)
    ang = lam_im * dt
    lb_re = mag * jnp.cos(ang)
    lb_im = mag * jnp.sin(ang)
    den = lam_re * lam_re + lam_im * lam_im
    nr = lb_re - 1.0
    coef_re = (nr * lam_re + lb_im * lam_im) / den
    coef_im = (lb_im * lam_re - nr * lam_im) / den
    b_re = b_re_ref[...]
    b_im = b_im_ref[...]
    bb = (coef_re * b_re - coef_im * b_im, coef_re * b_im + coef_im * b_re)
    cm = (c_re_ref[...], -c_im_ref[...])
    lbs = (lb_re, lb_im)
    for q in range(N_QUADS):
        gs = slice(q * QUAD_GROUPS, (q + 1) * QUAD_GROUPS)
        for part in range(2):
            k = 2 * q + part
            lb_ref[:, k * QUAD_STATE:(k + 1) * QUAD_STATE] = jnp.concatenate(
                [lbs[part][g] for g in range(q * QUAD_GROUPS, (q + 1) * QUAD_GROUPS)], axis=-1)
            x = bb[part][gs].reshape(QUAD_GROUPS * S5_GROUP, S5_STATE)
            bc_ref[k] = _bf16(_block_diag_rows(x, QUAD_GROUPS, S5_STATE))
            y = cm[part][gs].reshape(QUAD_GROUPS * S5_GROUP, S5_STATE)
            cc_ref[q, part * QUAD_STATE:(part + 1) * QUAD_STATE, :] = _bf16(
                _block_diag_rows_t(y, QUAD_GROUPS, S5_STATE))
    d_o_ref[...] = jnp.concatenate([d_ref[g:g + 1, :] for g in range(S5_GROUPS)], axis=-1)
    for src, dst in ((wa_ref, wa_o_ref), (wx_ref, wx_o_ref)):
        w = src[...].reshape(LRU_HEADS * LRU_HEAD_DIM, LRU_HEAD_DIM)
        for gt in range(GATE_TILES):
            dst[gt] = _bf16(_block_diag_rows(w[gt * MXU_DIM:(gt + 1) * MXU_DIM], HEADS_PER_TILE,
                                             LRU_HEAD_DIM))
    for src, dst in ((ba_ref, ba_o_ref), (bx_ref, bx_o_ref)):
        dst[...] = jnp.concatenate([src[h:h + 1, :] for h in range(LRU_HEADS)], axis=-1)
    nl = -lru_lam_ref[...]
    sp = jnp.maximum(nl, 0.0) + jnp.log1p(jnp.exp(-jnp.abs(nl)))
    c_lru_ref[...] = -LRU_C * sp


def _prep(lam_re, lam_im, log_dt, b_re, b_im, c_re, c_im, d, wa, wx, ba, bx, lru_lam):
    g, p = S5_GROUPS, S5_STATE
    f = jnp.float32
    bf = jnp.bfloat16
    return pl.pallas_call(
        _prep_kernel,
        out_shape=(jax.ShapeDtypeStruct((1, S5_COLS), f),
                   jax.ShapeDtypeStruct((2 * N_QUADS, LANES, QUAD_STATE), bf),
                   jax.ShapeDtypeStruct((N_QUADS, 2 * QUAD_STATE, LANES), bf),
                   jax.ShapeDtypeStruct((1, S5_WIDTH), f),
                   jax.ShapeDtypeStruct((GATE_TILES, MXU_DIM, MXU_DIM), bf),
                   jax.ShapeDtypeStruct((GATE_TILES, MXU_DIM, MXU_DIM), bf),
                   jax.ShapeDtypeStruct((1, LRU_WIDTH), f), jax.ShapeDtypeStruct((1, LRU_WIDTH), f),
                   jax.ShapeDtypeStruct((1, LRU_WIDTH), f)),
        name="s5_lru_param_prep",
    )(lam_re.reshape(g, 1, p), lam_im.reshape(g, 1, p), log_dt.reshape(g, 1, 1),
      jnp.swapaxes(b_re, 1, 2), jnp.swapaxes(b_im, 1, 2), c_re, c_im, d, wa, wx, ba, bx,
      lru_lam.reshape(1, LRU_WIDTH))


_ITEM_SEQ = ("nin", "pa_", "pub", "s5i", "pga", "pzb", "gtm", "scn", "gat", "s5o", "pgb", "glu",
             "mpa", "lsc", "mpb", "out", "nou")
_ITEM_SKEW = 7


def _emission_order(n_sub):
    order = []
    for step in range(len(_ITEM_SEQ) + _ITEM_SKEW * (n_sub - 1)):
        for s in range(n_sub):
            idx = step - s * _ITEM_SKEW
            if 0 <= idx < len(_ITEM_SEQ):
                order.append((_ITEM_SEQ[idx], s))
    return order


def _sub_tile_segments(s, rows_sub, nb, t_len):
    tile_rows = t_len * SUBLANES
    r0 = s * rows_sub
    if rows_sub <= tile_rows:
        j, off = divmod(r0, tile_rows)
        t0 = off // SUBLANES
        return r0, [(j, t0, t0 + rows_sub // SUBLANES)]
    return r0, [(j, 0, t_len) for j in range(r0 // tile_rows, (r0 + rows_sub) // tile_rows)]


def _layer_kernel(x_ref, *refs, nb, t_len, n_sub, zero_state):
    if zero_state:
        s5_0_ref = lru_0_ref = conv_0_ref = None
    else:
        s5_0_ref, lru_0_ref, conv_0_ref = refs[:3]
        refs = refs[3:]
    (lng_ref, w_in_ref, lb_ref, bc_ref, cc_ref, d_ref, w_glu_ref, b_glu_ref, conv_w_ref,
     conv_b_ref, wa_ref, ba_ref, wx_ref, bx_ref, c_lru_ref, w_pa_ref, w_pb_ref, w_out_ref, fg_ref,
     y_ref, s5_n_ref, lru_n_ref, conv_n_ref,
     tr_ref, hb_ref, ws_ref, gp_ref, cv_ref, hs5_ref, hl_ref) = refs
    tile_rows = t_len * SUBLANES
    tail_rows = (CONV_W - 1) * SUBLANES
    rows_sub = nb * tile_rows // n_sub
    g_cols = slice(LRU_WIDTH, 2 * LRU_WIDTH)
    k = pl.program_id(1)

    @pl.when(k == 0)
    def _init_state():
        if zero_state:
            hs5_ref[...] = jnp.zeros_like(hs5_ref)
            hl_ref[...] = jnp.zeros_like(hl_ref)
            cv_ref[:, 0:tail_rows, :] = jnp.zeros((nb, tail_rows, LRU_WIDTH), jnp.float32)
        else:
            hs5_ref[...] = s5_0_ref[...]
            hl_ref[...] = lru_0_ref[...]
            for j in range(nb):
                cv_ref[j, 0:tail_rows, :] = conv_0_ref[
                    :, j * SUBLANES:(j + 1) * SUBLANES, :].reshape(tail_rows, LRU_WIDTH)

    def sub(s):
        r0, segs = _sub_tile_segments(s, rows_sub, nb, t_len)
        return slice(r0, r0 + rows_sub), segs

    def wide_dot(lhs, w_ref, c0, n_chunks):
        parts = [_dot(lhs, w_ref[c0 + c]) for c in range(n_chunks)]
        return parts[0] if n_chunks == 1 else jnp.concatenate(parts, axis=-1)

    def proj(rows, col, width):
        return wide_dot(hb_ref[rows, :], w_in_ref, col // W_CHUNK, width // W_CHUNK)

    def it_norm_in(s, st):
        rows, segs = sub(s)
        for j, t0, t1 in segs:
            for b in range(SUBLANES):
                xb = x_ref[j * SUBLANES + b, t0:t1, :]
                inv = lax.rsqrt(jnp.mean(xb * xb, axis=-1, keepdims=True) + EPS)
                hn = xb * inv * lng_ref[...]
                for c in range(N_SLABS):
                    tr_ref[c, pl.ds((j * t_len + t0) * SUBLANES + b, t1 - t0, stride=SUBLANES), :] = (
                        hn[:, c * LANES:(c + 1) * LANES])
        for c in range(N_SLABS):
            hb_ref[rows, c * LANES:(c + 1) * LANES] = _bf16(tr_ref[c, rows, :])

    def it_proj_a(s, st):
        rows, _ = sub(s)
        pz = proj(rows, C_UA, 2 * S5_WIDTH)
        st["ua"] = pz[:, :S5_WIDTH]
        za = pz[:, S5_WIDTH:]
        st["sza"] = _silu(za)

    def it_conv(s, st):
        rows, segs = sub(s)
        ub = proj(rows, C_UB, LRU_WIDTH)
        xcs = []
        off = 0
        for j, t0, t1 in segs:
            n = (t1 - t0) * SUBLANES
            cv_ref[j, tail_rows + t0 * SUBLANES:tail_rows + t1 * SUBLANES, :] = ub[off:off + n]
            off += n
        for j, t0, t1 in segs:
            n = (t1 - t0) * SUBLANES
            xc = conv_b_ref[...] + conv_w_ref[0:1, :] * cv_ref[j, t0 * SUBLANES:t0 * SUBLANES + n, :]
            for kk in range(1, CONV_W):
                lo = (t0 + kk) * SUBLANES
                xc = xc + conv_w_ref[kk:kk + 1, :] * cv_ref[j, lo:lo + n, :]
            xcs.append(xc)
        st["xc"] = xcs[0] if len(xcs) == 1 else jnp.concatenate(xcs, axis=0)

    def it_s5_in(s, st):
        rows, _ = sub(s)
        ua_b = _bf16(st["ua"])
        for q in range(2 * N_QUADS):
            ws_ref[rows, q * QUAD_STATE:(q + 1) * QUAD_STATE] = _dot(
                ua_b[:, (q // 2) * LANES:(q // 2 + 1) * LANES], bc_ref[q])

    def it_s5_scan(s, st):
        _, segs = sub(s)
        for j, t0, t1 in segs:
            base = j * tile_rows
            for q in range(N_QUADS):
                re0 = q * 2 * QUAD_STATE
                im0 = re0 + QUAD_STATE
                lbr = jnp.broadcast_to(lb_ref[:, re0:re0 + QUAD_STATE], (SUBLANES, QUAD_STATE))
                lbi = jnp.broadcast_to(lb_ref[:, im0:im0 + QUAD_STATE], (SUBLANES, QUAD_STATE))
                hr = hs5_ref[j * SUBLANES:(j + 1) * SUBLANES, re0:re0 + QUAD_STATE]
                hi = hs5_ref[j * SUBLANES:(j + 1) * SUBLANES, im0:im0 + QUAD_STATE]
                for t in range(t0, t1):
                    row = slice(base + t * SUBLANES, base + (t + 1) * SUBLANES)
                    xr = ws_ref[row, re0:re0 + QUAD_STATE]
                    xi = ws_ref[row, im0:im0 + QUAD_STATE]
                    hr, hi = lbr * hr - lbi * hi + xr, lbr * hi + lbi * hr + xi
                    ws_ref[row, re0:re0 + QUAD_STATE] = hr
                    ws_ref[row, im0:im0 + QUAD_STATE] = hi
                hs5_ref[j * SUBLANES:(j + 1) * SUBLANES, re0:re0 + QUAD_STATE] = hr
                hs5_ref[j * SUBLANES:(j + 1) * SUBLANES, im0:im0 + QUAD_STATE] = hi

    def it_gate_dots(s, st):
        rows, _ = sub(s)
        xc_b = _bf16(st["xc"])
        for gt in range(GATE_TILES):
            cs = slice(gt * MXU_DIM, (gt + 1) * MXU_DIM)
            gp_ref[rows, cs] = _dot(xc_b[:, cs], wa_ref[gt])
            gp_ref[rows, LRU_WIDTH + gt * MXU_DIM:LRU_WIDTH + (gt + 1) * MXU_DIM] = _dot(
                xc_b[:, cs], wx_ref[gt])

    def it_gates(s, st):
        rows, _ = sub(s)
        xc = st["xc"]
        tr = jnp.tanh(0.5 * (gp_ref[rows, 0:LRU_WIDTH] + ba_ref[...]))
        ti = jnp.tanh(0.5 * (gp_ref[rows, LRU_WIDTH:2 * LRU_WIDTH] + bx_ref[...]))
        c4 = 0.25 * c_lru_ref[...]
        t = jnp.tanh(c4 + c4 * tr)
        ri = 1.0 / (1.0 - t)
        nt = -t
        root = jnp.where(nt > 0.0, nt * lax.rsqrt(nt), 0.0)
        st["a"] = (1.0 + t) * ri
        st["g"] = (root * ri) * (xc + xc * ti)

    def it_proj_ga(s, st):
        rows, _ = sub(s)
        st["sga"] = _sigmoid(proj(rows, C_GA, D_MODEL))

    def it_proj_zb(s, st):
        rows, _ = sub(s)
        zb = proj(rows, C_ZB, LRU_WIDTH)
        st["szb"] = _silu(zb)

    def it_proj_gb(s, st):
        rows, _ = sub(s)
        st["sgb"] = _sigmoid(proj(rows, C_GB, D_MODEL))

    def it_s5_out(s, st):
        rows, _ = sub(s)
        parts = []
        for q in range(N_QUADS):
            sq = _bf16(ws_ref[rows, q * 2 * QUAD_STATE:(q + 1) * 2 * QUAD_STATE])
            parts.append(_dot(sq, cc_ref[q]))
        ya = jnp.concatenate(parts, axis=-1) + st["ua"] * d_ref[...]
        st["ya"] = jax.nn.gelu(ya)

    def it_glu(s, st):
        ya = st["ya"]
        ya = ya * _sigmoid(_dot(_bf16(ya), w_glu_ref[...]) + b_glu_ref[...])
        st["ya2"] = _bf16(ya * st["sza"])

    def it_lru_scan(s, st):
        _, segs = sub(s)
        a, g = st["a"], st["g"]
        off = 0
        for j, t0, t1 in segs:
            base = j * tile_rows
            h = hl_ref[j * SUBLANES:(j + 1) * SUBLANES, :]
            for t in range(t0, t1):
                lo = off + (t - t0) * SUBLANES
                h = a[lo:lo + SUBLANES] * h + g[lo:lo + SUBLANES]
                ws_ref[base + t * SUBLANES:base + (t + 1) * SUBLANES, g_cols] = h
            off += (t1 - t0) * SUBLANES
            hl_ref[j * SUBLANES:(j + 1) * SUBLANES, :] = h

    def it_merge_a(s, st):
        pa = wide_dot(st["ya2"], w_pa_ref, 0, D_MODEL // W_CHUNK)
        st["m"] = st["sga"] * pa

    def it_merge_b(s, st):
        rows, _ = sub(s)
        hz = _bf16(ws_ref[rows, g_cols] * st["szb"])
        pb = wide_dot(hz, w_pb_ref, 0, D_MODEL // W_CHUNK)
        st["mb"] = _bf16(st["m"] + st["sgb"] * pb)

    def it_out(s, st):
        rows, _ = sub(s)
        delta = wide_dot(st["mb"], w_out_ref, 0, D_MODEL // W_CHUNK)
        for c in range(N_SLABS):
            tr_ref[c, rows, :] = delta[:, c * LANES:(c + 1) * LANES]

    def it_norm_out(s, st):
        _, segs = sub(s)
        for j, t0, t1 in segs:
            for b in range(SUBLANES):
                parts = [tr_ref[c, pl.ds((j * t_len + t0) * SUBLANES + b, t1 - t0, stride=SUBLANES), :]
                         for c in range(N_SLABS)]
                o = x_ref[j * SUBLANES + b, t0:t1, :] + jnp.concatenate(parts, axis=-1)
                inv = lax.rsqrt(jnp.mean(o * o, axis=-1, keepdims=True) + EPS)
                y_ref[j * SUBLANES + b, t0:t1, :] = o * inv * fg_ref[...]

    items = dict(nin=it_norm_in, pa_=it_proj_a, pub=it_conv, s5i=it_s5_in, scn=it_s5_scan,
                 gtm=it_gate_dots, gat=it_gates, pga=it_proj_ga, pzb=it_proj_zb, pgb=it_proj_gb,
                 s5o=it_s5_out, glu=it_glu, lsc=it_lru_scan, mpa=it_merge_a, mpb=it_merge_b,
                 out=it_out, nou=it_norm_out)
    state = [dict() for _ in range(n_sub)]
    for name, s in _emission_order(n_sub):
        items[name](s, state[s])

    s5_n_ref[...] = hs5_ref[...]
    lru_n_ref[...] = hl_ref[...]
    for j in range(nb):
        new_tail = cv_ref[j, tile_rows:tile_rows + tail_rows, :]
        cv_ref[j, 0:tail_rows, :] = new_tail
        conv_n_ref[:, j * SUBLANES:(j + 1) * SUBLANES, :] = new_tail.reshape(
            CONV_W - 1, SUBLANES, LRU_WIDTH)


def _full_spec(arr):
    zeros = (0,) * arr.ndim
    return pl.BlockSpec(arr.shape, lambda i, k, zeros=zeros: zeros)


def _tiling(bsz, seq):
    t_len = min(seq, STEP_ROWS // SUBLANES)
    nb = STEP_ROWS // (t_len * SUBLANES)
    assert t_len % SUBLANES == 0 and seq % t_len == 0 and bsz % (nb * SUBLANES) == 0
    return nb, t_len, STEP_ROWS // SUB_TILE_ROWS


def _run_group(x, state, weights):
    bsz, seq, _ = x.shape
    nb, t_len, n_sub = _tiling(bsz, seq)
    bt = nb * SUBLANES
    rows = bt * t_len
    grid = (bsz // bt, seq // t_len)
    f = jnp.float32
    state_specs = [
        pl.BlockSpec((bt, S5_COLS), lambda i, k: (i, 0)),
        pl.BlockSpec((bt, LRU_WIDTH), lambda i, k: (i, 0)),
        pl.BlockSpec((CONV_W - 1, bt, LRU_WIDTH), lambda i, k: (0, i, 0)),
    ]
    zero_state = state is None
    state = () if zero_state else tuple(state)
    in_specs = ([pl.BlockSpec((bt, t_len, D_MODEL), lambda i, k: (i, k, 0))]
                + ([] if zero_state else state_specs) + [_full_spec(w) for w in weights])
    out_specs = (
        pl.BlockSpec((bt, t_len, D_MODEL), lambda i, k: (i, k, 0)),
        pl.BlockSpec((bt, S5_COLS), lambda i, k: (i, 0)),
        pl.BlockSpec((bt, LRU_WIDTH), lambda i, k: (i, 0)),
        pl.BlockSpec((CONV_W - 1, bt, LRU_WIDTH), lambda i, k: (0, i, 0)),
    )
    out_shape = (
        jax.ShapeDtypeStruct((bsz, seq, D_MODEL), x.dtype),
        jax.ShapeDtypeStruct((bsz, S5_COLS), f),
        jax.ShapeDtypeStruct((bsz, LRU_WIDTH), f),
        jax.ShapeDtypeStruct((CONV_W - 1, bsz, LRU_WIDTH), f),
    )
    scratch = [
        pltpu.VMEM((N_SLABS, rows, LANES), f),
        pltpu.VMEM((rows, D_MODEL), jnp.bfloat16),
        pltpu.VMEM((rows, S5_COLS), f),
        pltpu.VMEM((rows, 2 * LRU_WIDTH), f),
        pltpu.VMEM((nb, (t_len + CONV_W - 1) * SUBLANES, LRU_WIDTH), f),
        pltpu.VMEM((bt, S5_COLS), f),
        pltpu.VMEM((bt, LRU_WIDTH), f),
    ]
    return pl.pallas_call(
        functools.partial(_layer_kernel, nb=nb, t_len=t_len, n_sub=n_sub, zero_state=zero_state),
        grid=grid,
        in_specs=in_specs,
        out_specs=out_specs,
        out_shape=out_shape,
        scratch_shapes=scratch,
        compiler_params=pltpu.CompilerParams(
            dimension_semantics=("arbitrary", "arbitrary"),
            vmem_limit_bytes=VMEM_LIMIT_BYTES),
        name=f"hybrid_layer_nb{nb}_t{t_len}",
    )(x, *state, *weights)


def _cast_kernel(w_in_ref, w_glu_ref, w_pa_ref, w_pb_ref, w_out_ref,
                 o_in_ref, o_glu_ref, o_pa_ref, o_pb_ref, o_out_ref):
    def chunks(src_ref, dst_ref):
        for c in range(dst_ref.shape[0]):
            dst_ref[c] = _bf16(src_ref[:, c * W_CHUNK:(c + 1) * W_CHUNK])

    chunks(w_in_ref, o_in_ref)

    @pl.when(pl.program_id(0) == 0)
    def _():
        o_glu_ref[...] = _bf16(w_glu_ref[...])
        chunks(w_pa_ref, o_pa_ref)
        chunks(w_pb_ref, o_pb_ref)
        chunks(w_out_ref, o_out_ref)


def _cast_weights(w_in, w_glu, w_pa, w_pb, w_out):
    per_step = D_MODEL // W_CHUNK
    n_steps = w_in.shape[1] // D_MODEL

    def whole(shape):
        return pl.BlockSpec(shape, lambda i, n=len(shape): (0,) * n)

    def chunk_shape(w):
        return jax.ShapeDtypeStruct((w.shape[1] // W_CHUNK, w.shape[0], W_CHUNK), jnp.bfloat16)

    outs = (chunk_shape(w_in), jax.ShapeDtypeStruct(w_glu.shape, jnp.bfloat16),
            chunk_shape(w_pa), chunk_shape(w_pb), chunk_shape(w_out))
    return pl.pallas_call(
        _cast_kernel,
        grid=(n_steps,),
        in_specs=[pl.BlockSpec((w_in.shape[0], D_MODEL), lambda i: (0, i)), whole(w_glu.shape),
                  whole(w_pa.shape), whole(w_pb.shape), whole(w_out.shape)],
        out_specs=(pl.BlockSpec((per_step, w_in.shape[0], W_CHUNK), lambda i: (i, 0, 0)),
                   whole(outs[1].shape), whole(outs[2].shape), whole(outs[3].shape),
                   whole(outs[4].shape)),
        out_shape=outs,
        compiler_params=pltpu.CompilerParams(dimension_semantics=("arbitrary",),
                                             vmem_limit_bytes=VMEM_LIMIT_BYTES),
        name="weights_to_bf16_chunks",
    )(w_in, w_glu, w_pa, w_pb, w_out)


def _s5_state_to_cols(s):
    bsz = s.shape[0]
    s = s.reshape(bsz, N_QUADS, QUAD_GROUPS, S5_STATE, 2)
    return jnp.transpose(s, (0, 1, 4, 2, 3)).reshape(bsz, S5_COLS)


def _s5_cols_to_state(c):
    bsz = c.shape[0]
    c = c.reshape(bsz, N_QUADS, 2, QUAD_GROUPS, S5_STATE)
    return jnp.transpose(c, (0, 1, 3, 4, 2)).reshape(bsz, S5_GROUPS, S5_STATE, 2)


def kernel(x_prompt, x_sample, state_s5, state_lru, state_conv, ln_gain, w_in, s5_lambda_re, s5_lambda_im, s5_log_dt, s5_b_re, s5_b_im, s5_c_re, s5_c_im, s5_d, w_glu, b_glu, conv_w, conv_b, lru_wa, lru_ba, lru_wx, lru_bx, lru_lambda, w_pa, w_pb, w_out, final_gain):
    depth = ln_gain.shape[0]
    assert depth == 1
    l = 0
    f = jnp.float32
    lb, bc, cc, d_row, wa, wx, ba_row, bx_row, c_lru = _prep(
        s5_lambda_re[l].astype(f), s5_lambda_im[l].astype(f), s5_log_dt[l].astype(f),
        s5_b_re[l].astype(f), s5_b_im[l].astype(f), s5_c_re[l].astype(f), s5_c_im[l].astype(f),
        s5_d[l].astype(f), lru_wa[l].astype(f), lru_wx[l].astype(f), lru_ba[l].astype(f),
        lru_bx[l].astype(f), lru_lambda[l].astype(f))
    w_in_c, w_glu_c, w_pa_c, w_pb_c, w_out_c = _cast_weights(
        w_in[l].astype(f), w_glu[l].astype(f), w_pa[l].astype(f), w_pb[l].astype(f),
        w_out[l].astype(f))
    weights = (
        ln_gain[l].reshape(1, D_MODEL).astype(f), w_in_c, lb, bc, cc, d_row, w_glu_c,
        b_glu[l].reshape(1, S5_WIDTH).astype(f), conv_w[l].astype(f),
        conv_b[l].reshape(1, LRU_WIDTH).astype(f), wa, ba_row, wx, bx_row, c_lru,
        w_pa_c, w_pb_c, w_out_c, final_gain.reshape(1, D_MODEL).astype(f),
    )

    y_p, s5_p, lru_p, conv_p = _run_group(x_prompt, None, weights)
    sample_state = (_s5_state_to_cols(state_s5[l].astype(f)), state_lru[l].astype(f),
                    jnp.swapaxes(state_conv[l].astype(f), 0, 1))
    y_s, s5_s, lru_s, conv_s = _run_group(x_sample, sample_state, weights)

    sdt = state_s5.dtype
    return (y_p, y_s,
            _s5_cols_to_state(s5_p)[None].astype(sdt), lru_p[None].astype(state_lru.dtype),
            jnp.swapaxes(conv_p, 0, 1)[None].astype(state_conv.dtype),
            _s5_cols_to_state(s5_s)[None].astype(sdt), lru_s[None].astype(state_lru.dtype),
            jnp.swapaxes(conv_s, 0, 1)[None].astype(state_conv.dtype))
```
